```python
import jax, jax.numpy as jnp
from jax import lax
import numpy as np

D_MODEL = 2048
BATCH = 4
SEQ = 4096
DEPTH = 1

CHUNK = 64
D_MIX = D_MODEL
CONV_WIDTH = D_MIX // 2
CONV_GROUPS = 8
CONV_KERNEL = 31
DN_HEADS = 8
DN_HEAD_DIM = (D_MIX - CONV_WIDTH) // DN_HEADS
DN_WIDTH = DN_HEADS * DN_HEAD_DIM
SHORT_CONV = 4
D_FF = ((8 * D_MODEL // 3 + 255) // 256) * 256
N_MOD = 9
EPS = 1e-6
IN_SPLITS = [CONV_WIDTH, CONV_WIDTH, DN_WIDTH, DN_WIDTH, DN_WIDTH, DN_WIDTH, DN_HEADS, DN_HEADS]
IN_COLS = sum(IN_SPLITS)

kernel_name = "hybrid_conv_gdn_macaron_adaln"


def rmsnorm(x, w):
    xf = x.astype(jnp.float32)
    y = xf * lax.rsqrt(jnp.mean(xf * xf, axis=-1, keepdims=True) + EPS)
    return (y * w.astype(jnp.float32)).astype(x.dtype)


def modulate(h, shift, scale):
    return h * (1 + scale[:, None, :]) + shift[:, None, :]


def swiglu(h, wg, wu, wd):
    return (jax.nn.silu(h @ wg) * (h @ wu)) @ wd


def causal_dwconv(x, w):
    k = w.shape[0]
    return lax.conv_general_dilated(
        x, w[:, None, :].astype(x.dtype), window_strides=(1,), padding=[(k - 1, 0)],
        dimension_numbers=("NWC", "WIO", "NWC"), feature_group_count=x.shape[-1])


def l2norm(t):
    return t * lax.rsqrt(jnp.sum(t * t, axis=-1, keepdims=True) + EPS)


def conformer_conv_group(a, gate, w_dw, b_dw, ln_w, ln_b):
    h = a * jax.nn.sigmoid(gate)
    h = causal_dwconv(h, w_dw) + b_dw.astype(h.dtype)
    hf = h.astype(jnp.float32)
    mu = jnp.mean(hf, axis=-1, keepdims=True)
    var = jnp.mean(jnp.square(hf - mu), axis=-1, keepdims=True)
    hf = (hf - mu) * lax.rsqrt(var + EPS) * ln_w.astype(jnp.float32) + ln_b.astype(jnp.float32)
    return jax.nn.silu(hf).astype(a.dtype)


def gated_deltanet_group(q, k, v, z, b_raw, a_raw, w_short, a_log, dt_bias, onorm_w):
    B, S, _ = q.shape
    H, Dh, C = DN_HEADS, DN_HEAD_DIM, CHUNK
    N = S // C
    f32 = jnp.float32
    qkv = jax.nn.silu(causal_dwconv(jnp.concatenate([q, k, v], axis=-1), w_short))
    q, k, v = jnp.split(qkv.astype(f32), 3, axis=-1)

    def heads(t):
        return t.reshape(B, N, C, H, Dh).transpose(0, 3, 1, 2, 4)

    def hc(t):
        return t.reshape(B, N, C, H).transpose(0, 3, 1, 2)

    q = l2norm(heads(q)) * (Dh ** -0.5)
    k = l2norm(heads(k))
    v = heads(v)
    beta = hc(jax.nn.sigmoid(b_raw.astype(f32)))
    g = hc(-jnp.exp(a_log.astype(f32)) * jax.nn.softplus(a_raw.astype(f32) + dt_bias.astype(f32)))
    G = jnp.cumsum(g, axis=-1)

    idx = jnp.arange(C)
    tril = idx[:, None] >= idx[None, :]
    strict = idx[:, None] > idx[None, :]
    decay = jnp.exp(jnp.where(tril, G[..., :, None] - G[..., None, :], -jnp.inf))

    kb = k * beta[..., None]
    L = jnp.einsum("bhncd,bhnsd->bhncs", kb, k) * jnp.where(strict, decay, 0.0)
    rhs = jnp.concatenate([v * beta[..., None], kb * jnp.exp(G)[..., None]], axis=-1)
    sol = lax.linalg.triangular_solve(L + jnp.eye(C, dtype=f32), rhs, left_side=True,
                                      lower=True, unit_diagonal=True)
    u, w = jnp.split(sol, 2, axis=-1)
    a_intra = jnp.einsum("bhncd,bhnsd->bhncs", q, k) * decay

    def step(state, xs):
        q_n, k_n, u_n, w_n, a_n, g_n = xs
        v_new = u_n - jnp.einsum("bhcd,bhde->bhce", w_n, state)
        o = (jnp.einsum("bhcd,bhde->bhce", q_n * jnp.exp(g_n)[..., None], state)
             + jnp.einsum("bhcs,bhse->bhce", a_n, v_new))
        g_last = g_n[..., -1]
        state = (state * jnp.exp(g_last)[..., None, None]
                 + jnp.einsum("bhcd,bhce->bhde", k_n * jnp.exp(g_last[..., None] - g_n)[..., None], v_new))
        return state, o

    xs = tuple(jnp.moveaxis(t, 2, 0) for t in (q, k, u, w, a_intra, G))
    state0 = jnp.zeros((B, H, Dh, Dh), f32)
    _, o = lax.scan(step, state0, xs)
    o = o.transpose(1, 0, 3, 2, 4).reshape(B, S, H, Dh)
    o = o * lax.rsqrt(jnp.mean(o * o, axis=-1, keepdims=True) + EPS) * onorm_w.astype(f32)
    o = o * jax.nn.silu(z.astype(f32).reshape(B, S, H, Dh))
    return o.reshape(B, S, DN_WIDTH).astype(z.dtype)


def setup_inputs(seed: int = 0) -> dict:
    key = jax.random.key(seed)
    ks = jax.random.split(key, 32)
    f32 = jnp.float32
    nrm = lambda k, shape, s: jax.random.normal(k, shape, f32) * s
    D, L = D_MODEL, DEPTH
    dt = jnp.exp(jax.random.uniform(ks[16], (L, DN_HEADS), f32, np.log(1e-3), np.log(1e-1)))
    return {
        "x": nrm(ks[0], (BATCH, SEQ, D), 1.0),
        "c": nrm(ks[1], (BATCH, D), 1.0),
        "w_ada": nrm(ks[2], (L, D, N_MOD * D), D ** -0.5),
        "b_ada": nrm(ks[3], (L, N_MOD * D), 0.01),
        "ffn1_norm": 1.0 + nrm(ks[4], (L, D), 0.01),
        "ffn1_wg": nrm(ks[5], (L, D, D_FF), D ** -0.5),
        "ffn1_wu": nrm(ks[6], (L, D, D_FF), D ** -0.5),
        "ffn1_wd": nrm(ks[7], (L, D_FF, D), D_FF ** -0.5),
        "mix_norm": 1.0 + nrm(ks[8], (L, D), 0.01),
        "w_in": nrm(ks[9], (L, D, IN_COLS), D ** -0.5),
        "w_dw": nrm(ks[10], (L, CONV_KERNEL, CONV_WIDTH), CONV_KERNEL ** -0.5),
        "b_dw": nrm(ks[11], (L, CONV_WIDTH), 0.01),
        "conv_ln_w": 1.0 + nrm(ks[12], (L, CONV_WIDTH), 0.01),
        "conv_ln_b": nrm(ks[13], (L, CONV_WIDTH), 0.01),
        "w_short": nrm(ks[14], (L, SHORT_CONV, 3 * DN_WIDTH), SHORT_CONV ** -0.5),
        "a_log": jnp.log(jax.random.uniform(ks[15], (L, DN_HEADS), f32, 1.0, 16.0)),
        "dt_bias": dt + jnp.log(-jnp.expm1(-dt)),
        "dn_norm_w": 1.0 + nrm(ks[17], (L, DN_HEAD_DIM), 0.01),
        "w_out": nrm(ks[18], (L, D_MIX, D), D_MIX ** -0.5),
        "ffn2_norm": 1.0 + nrm(ks[19], (L, D), 0.01),
        "ffn2_wg": nrm(ks[20], (L, D, D_FF), D ** -0.5),
        "ffn2_wu": nrm(ks[21], (L, D, D_FF), D ** -0.5),
        "ffn2_wd": nrm(ks[22], (L, D_FF, D), D_FF ** -0.5),
        "final_norm": 1.0 + nrm(ks[23], (D,), 0.01),
    }


def reference(x, c, w_ada, b_ada, ffn1_norm, ffn1_wg, ffn1_wu, ffn1_wd, mix_norm, w_in,
              w_dw, b_dw, conv_ln_w, conv_ln_b, w_short, a_log, dt_bias, dn_norm_w, w_out,
              ffn2_norm, ffn2_wg, ffn2_wu, ffn2_wd, final_norm):
    B = x.shape[0]
    split_idx = list(np.cumsum(IN_SPLITS)[:-1])
    for l in range(DEPTH):
        mods = (jax.nn.silu(c) @ w_ada[l] + b_ada[l]).reshape(B, N_MOD, D_MODEL)
        h = modulate(rmsnorm(x, ffn1_norm[l]), mods[:, 0], mods[:, 1])
        x = x + 0.5 * mods[:, 2][:, None, :] * swiglu(h, ffn1_wg[l], ffn1_wu[l], ffn1_wd[l])
        h = modulate(rmsnorm(x, mix_norm[l]), mods[:, 3], mods[:, 4])
        p = h @ w_in[l]
        ca, cg, q, k, v, z, b_raw, a_raw = jnp.split(p, split_idx, axis=-1)
        y_conv = conformer_conv_group(ca, cg, w_dw[l], b_dw[l], conv_ln_w[l], conv_ln_b[l])
        y_dn = gated_deltanet_group(q, k, v, z, b_raw, a_raw, w_short[l], a_log[l],
                                    dt_bias[l], dn_norm_w[l])
        y = jnp.concatenate([y_conv, y_dn], axis=-1) @ w_out[l]
        x = x + mods[:, 5][:, None, :] * y
        h = modulate(rmsnorm(x, ffn2_norm[l]), mods[:, 6], mods[:, 7])
        x = x + 0.5 * mods[:, 8][:, None, :] * swiglu(h, ffn2_wg[l], ffn2_wu[l], ffn2_wd[l])
    return rmsnorm(x, final_norm)
```

```python
import functools

import jax
import jax.numpy as jnp
from jax import lax
from jax.experimental import pallas as pl
from jax.experimental.pallas import tpu as pltpu

F32 = jnp.float32
BF16 = jnp.bfloat16

EPS = 1e-6
N_MOD = 9
DN_CHUNK = 64
DN_TILE = 2 * DN_CHUNK
INV_BASE = 8
LANES = 128
HALO = 8
CONV_HALO = 32
VMEM_LIMIT = 56 * 1024 * 1024


def _tile(n, pref):
    t = min(n, pref)
    assert n % t == 0, (n, pref)
    return t


def _sigmoid(v):
    return jax.nn.sigmoid(v)


def _silu(v):
    return v * _sigmoid(v)


def _bdot(a, b):
    return jnp.dot(a.astype(BF16), b.astype(BF16), preferred_element_type=F32)


def _norm_mod(x, nw, shift, scale):
    ms = jnp.mean(x * x, axis=-1, keepdims=True)
    y = x * lax.rsqrt(ms + EPS) * nw
    return y * (1.0 + scale) + shift


def _mods_kernel(c_ref, w_ref, b_ref, o_ref):
    s = _silu(c_ref[...])
    o_ref[...] = _bdot(s, w_ref[...]) + b_ref[...]


def _mods(c_pad, w_ada, b_ada):
    rows, d = c_pad.shape
    n = w_ada.shape[1]
    tn = _tile(d, 1024)
    assert n % tn == 0
    return pl.pallas_call(
        _mods_kernel,
        grid=(n // tn,),
        in_specs=[
            pl.BlockSpec((rows, d), lambda j: (0, 0)),
            pl.BlockSpec((d, tn), lambda j: (0, j)),
            pl.BlockSpec((1, tn), lambda j: (0, j)),
        ],
        out_specs=pl.BlockSpec((rows, tn), lambda j: (0, j)),
        out_shape=jax.ShapeDtypeStruct((rows, n), F32),
        compiler_params=pltpu.CompilerParams(
            dimension_semantics=("parallel",), vmem_limit_bytes=VMEM_LIMIT),
        name="mods",
    )(c_pad, w_ada, b_ada)


def _ffn_kernel(x_ref, mods_ref, nw_ref, wg_ref, wu_ref, wd_ref, *rest, mod_base, final):
    if final:
        fnw_ref, o_ref, h_scr, acc_scr = rest
    else:
        o_ref, h_scr, acc_scr = rest
    f = pl.program_id(1)

    @pl.when(f == 0)
    def _():
        shift = mods_ref[0, mod_base:mod_base + 1, :]
        scale = mods_ref[0, mod_base + 1:mod_base + 2, :]
        h_scr[...] = _norm_mod(x_ref[...], nw_ref[...], shift, scale).astype(BF16)
        acc_scr[...] = jnp.zeros_like(acc_scr)

    h = h_scr[...]
    g = jnp.dot(h, wg_ref[...], preferred_element_type=F32)
    u = jnp.dot(h, wu_ref[...], preferred_element_type=F32)
    a = (_silu(g) * u).astype(BF16)
    acc_scr[...] += jnp.dot(a, wd_ref[...], preferred_element_type=F32)

    @pl.when(f == pl.num_programs(1) - 1)
    def _():
        gate = mods_ref[0, mod_base + 2:mod_base + 3, :]
        y = x_ref[...] + 0.5 * gate * acc_scr[...]
        if final:
            ms = jnp.mean(y * y, axis=-1, keepdims=True)
            y = y * lax.rsqrt(ms + EPS) * fnw_ref[...]
        o_ref[...] = y


def _ffn(x2, mods, nw, wg, wu, wd, fnw, *, mod_base, seq, tm=512, tf=512):
    m, d = x2.shape
    dff = wg.shape[1]
    tm = _tile(seq, tm)
    tf = _tile(dff, tf)
    tiles_per_batch = seq // tm
    final = fnw is not None
    in_specs = [
        pl.BlockSpec((tm, d), lambda i, f: (i, 0)),
        pl.BlockSpec((1, N_MOD, d), lambda i, f: (i // tiles_per_batch, 0, 0)),
        pl.BlockSpec((1, d), lambda i, f: (0, 0)),
        pl.BlockSpec((d, tf), lambda i, f: (0, f)),
        pl.BlockSpec((d, tf), lambda i, f: (0, f)),
        pl.BlockSpec((tf, d), lambda i, f: (f, 0)),
    ]
    args = [x2, mods, nw, wg, wu, wd]
    if final:
        in_specs.append(pl.BlockSpec((1, d), lambda i, f: (0, 0)))
        args.append(fnw)
    return pl.pallas_call(
        functools.partial(_ffn_kernel, mod_base=mod_base, final=final),
        grid=(m // tm, dff // tf),
        in_specs=in_specs,
        out_specs=pl.BlockSpec((tm, d), lambda i, f: (i, 0)),
        out_shape=jax.ShapeDtypeStruct((m, d), F32),
        scratch_shapes=[pltpu.VMEM((tm, d), BF16), pltpu.VMEM((tm, d), F32)],
        compiler_params=pltpu.CompilerParams(
            dimension_semantics=("parallel", "arbitrary"), vmem_limit_bytes=VMEM_LIMIT),
        name="ffn_final" if final else "ffn",
    )(*args)


def _inproj_kernel(x_ref, mods_ref, nw_ref, w_ref, ws_ref, p_ref, ba_ref, h_scr, *, mod_base):
    n = pl.program_id(1)

    @pl.when(n == 0)
    def _():
        shift = mods_ref[0, mod_base:mod_base + 1, :]
        scale = mods_ref[0, mod_base + 1:mod_base + 2, :]
        h = _norm_mod(x_ref[...], nw_ref[...], shift, scale).astype(BF16)
        h_scr[...] = h
        ba_ref[...] = jnp.dot(h, ws_ref[...], preferred_element_type=F32)

    p_ref[...] = jnp.dot(h_scr[...], w_ref[...], preferred_element_type=F32)


def _inproj(x2, mods, nw, w_main, w_small, *, mod_base, seq, tm=512, tn=1024):
    m, d = x2.shape
    ncols = w_main.shape[1]
    tm = _tile(seq, tm)
    tn = min(tn, ncols // 6)
    assert ncols % tn == 0
    tiles_per_batch = seq // tm
    return pl.pallas_call(
        functools.partial(_inproj_kernel, mod_base=mod_base),
        grid=(m // tm, ncols // tn),
        in_specs=[
            pl.BlockSpec((tm, d), lambda i, n: (i, 0)),
            pl.BlockSpec((1, N_MOD, d), lambda i, n: (i // tiles_per_batch, 0, 0)),
            pl.BlockSpec((1, d), lambda i, n: (0, 0)),
            pl.BlockSpec((d, tn), lambda i, n: (0, n)),
            pl.BlockSpec((d, LANES), lambda i, n: (0, 0)),
        ],
        out_specs=[
            pl.BlockSpec((tm, tn), lambda i, n: (i, n)),
            pl.BlockSpec((tm, LANES), lambda i, n: (i, 0)),
        ],
        out_shape=[
            jax.ShapeDtypeStruct((m, ncols), F32),
            jax.ShapeDtypeStruct((m, LANES), F32),
        ],
        scratch_shapes=[pltpu.VMEM((tm, d), BF16)],
        compiler_params=pltpu.CompilerParams(
            dimension_semantics=("parallel", "arbitrary"), vmem_limit_bytes=VMEM_LIMIT),
        name="in_proj",
    )(x2, mods, nw, w_main, w_small)


def _conv_kernel(ca_ref, cg_ref, wdw_ref, bdw_ref, lnw_ref, lnb_ref, y_ref, buf, *, taps, tc, rb):
    s = pl.program_id(1)

    @pl.when(s == 0)
    def _():
        buf[0:CONV_HALO, :] = jnp.zeros((CONV_HALO, buf.shape[1]), F32)

    buf[CONV_HALO:CONV_HALO + tc, :] = ca_ref[0] * _sigmoid(cg_ref[0])
    first = CONV_HALO - (taps - 1)
    for r in range(0, tc, rb):
        acc = wdw_ref[0:1, :] * buf[first + r:first + r + rb, :]
        for j in range(1, taps):
            acc = acc + wdw_ref[j:j + 1, :] * buf[first + r + j:first + r + j + rb, :]
        hh = acc + bdw_ref[...]
        mu = jnp.mean(hh, axis=-1, keepdims=True)
        cen = hh - mu
        var = jnp.mean(cen * cen, axis=-1, keepdims=True)
        hn = cen * lax.rsqrt(var + EPS) * lnw_ref[...] + lnb_ref[...]
        y_ref[0, r:r + rb, :] = _silu(hn).astype(y_ref.dtype)
    buf[0:CONV_HALO, :] = buf[tc:tc + CONV_HALO, :]


def _conv_group(p3, w_dw, b_dw, ln_w, ln_b, *, tc=256, rb=32):
    b, s, _ = p3.shape
    taps, cw = w_dw.shape
    assert taps - 1 <= CONV_HALO
    tc = _tile(s, tc)
    rb = _tile(tc, rb)
    return pl.pallas_call(
        functools.partial(_conv_kernel, taps=taps, tc=tc, rb=rb),
        grid=(b, s // tc),
        in_specs=[
            pl.BlockSpec((1, tc, cw), lambda i, j: (i, j, 0)),
            pl.BlockSpec((1, tc, cw), lambda i, j: (i, j, 1)),
            pl.BlockSpec((taps, cw), lambda i, j: (0, 0)),
            pl.BlockSpec((1, cw), lambda i, j: (0, 0)),
            pl.BlockSpec((1, cw), lambda i, j: (0, 0)),
            pl.BlockSpec((1, cw), lambda i, j: (0, 0)),
        ],
        out_specs=pl.BlockSpec((1, tc, cw), lambda i, j: (i, j, 0)),
        out_shape=jax.ShapeDtypeStruct((b, s, cw), BF16),
        scratch_shapes=[pltpu.VMEM((tc + CONV_HALO, cw), F32)],
        compiler_params=pltpu.CompilerParams(
            dimension_semantics=("parallel", "arbitrary"), vmem_limit_bytes=VMEM_LIMIT),
        name="conv_group",
    )(p3, p3, w_dw, b_dw, ln_w, ln_b)


def _dn_kernel(q_ref, k_ref, v_ref, z_ref, ba_ref, wsh_ref, hp_ref, onw_ref, y_ref,
               xbuf, state, *, heads, dh, taps):
    t, c = DN_TILE, DN_CHUNK
    w = heads * dh
    s = pl.program_id(1)

    @pl.when(s == 0)
    def _():
        xbuf[0:HALO, :] = jnp.zeros((HALO, 3 * w), F32)
        state[...] = jnp.zeros_like(state)

    xbuf[HALO:HALO + t, 0:w] = q_ref[0]
    xbuf[HALO:HALO + t, w:2 * w] = k_ref[0]
    xbuf[HALO:HALO + t, 2 * w:3 * w] = v_ref[0]
    first = HALO - (taps - 1)

    ba = ba_ref[0]
    sig = _sigmoid(ba)
    xx = ba + hp_ref[1:2, :]
    softplus = jnp.maximum(xx, 0.0) + jnp.log1p(jnp.exp(-jnp.abs(xx)))
    gg = -jnp.exp(hp_ref[0:1, :]) * softplus

    row = lax.broadcasted_iota(jnp.int32, (t, t), 0)
    col = lax.broadcasted_iota(jnp.int32, (t, t), 1)
    same = (row // c) == (col // c)
    tril = same & (row >= col)
    strict = same & (row > col)
    eye = (row == col).astype(F32)
    blk = lambda n: (row // n) == (col // n)
    base_mask = blk(INV_BASE)
    merge_masks = []
    n = INV_BASE
    while n < c:
        merge_masks.append(blk(2 * n) & jnp.logical_not(blk(n)))
        n *= 2
    gcum = jnp.dot(tril.astype(F32), gg, preferred_element_type=F32,
                   precision=lax.Precision.HIGHEST)
    gcum_t = gcum.T

    for h in range(heads):
        cs = slice(h * dh, (h + 1) * dh)

        def short_conv(base):
            acc = wsh_ref[0:1, base + h * dh:base + (h + 1) * dh] * xbuf[first:first + t, base + h * dh:base + (h + 1) * dh]
            for j in range(1, taps):
                acc = acc + (wsh_ref[j:j + 1, base + h * dh:base + (h + 1) * dh]
                             * xbuf[first + j:first + j + t, base + h * dh:base + (h + 1) * dh])
            return _silu(acc)

        qh = short_conv(0)
        kh = short_conv(w)
        vh = short_conv(2 * w)
        qh = qh * lax.rsqrt(jnp.sum(qh * qh, axis=-1, keepdims=True) + EPS) * (dh ** -0.5)
        kh = kh * lax.rsqrt(jnp.sum(kh * kh, axis=-1, keepdims=True) + EPS)

        beta = sig[:, h:h + 1]
        g_col = gcum[:, heads + h:heads + h + 1]
        g_row = gcum_t[heads + h:heads + h + 1, :]
        diff = g_col - g_row
        decay = jnp.exp(jnp.where(tril, diff, -jnp.inf))
        decay_strict = jnp.where(strict, decay, 0.0)
        e_g = jnp.exp(g_col)

        kb = kh * beta
        aq = lax.dot_general(jnp.concatenate([kb, qh], axis=0).astype(BF16), kh.astype(BF16),
                             (((1,), (1,)), ((), ())), preferred_element_type=F32)
        a_intra = aq[t:] * decay
        lmat = aq[:t] * decay_strict
        xp = -jnp.where(base_mask, lmat, 0.0)
        rr = eye + xp
        xp = _bdot(xp, xp)
        for _ in range(INV_BASE.bit_length() - 3):
            pr = _bdot(jnp.concatenate([rr, xp], axis=0), xp)
            rr = rr + pr[:t]
            xp = pr[t:]
        rr = rr + _bdot(rr, xp)
        for off_mask in merge_masks:
            rr = rr - _bdot(_bdot(rr, jnp.where(off_mask, lmat, 0.0)), rr)

        rhs = jnp.concatenate([vh * beta, kb * e_g], axis=1)
        sol = _bdot(rr, rhs)
        u_all = sol[:, :dh]
        w_all = sol[:, dh:]
        qg = qh * e_g

        for ci in range(t // c):
            rows = slice(ci * c, (ci + 1) * c)
            st = state[h]
            wq = _bdot(jnp.concatenate([w_all[rows], qg[rows]], axis=0), st)
            v_new = u_all[rows] - wq[:c]
            zeros = jnp.zeros_like(v_new)
            parts = [zeros] * (t // c)
            parts[ci] = v_new
            v_pad = jnp.concatenate(parts, axis=0)
            o = wq[c:] + _bdot(a_intra[rows], v_pad)
            g_last = g_col[ci * c + c - 1:ci * c + c, :]
            kd = kh[rows] * jnp.exp(g_last - g_col[rows])
            upd = lax.dot_general(kd.astype(BF16), v_new.astype(BF16),
                                  (((0,), (0,)), ((), ())), preferred_element_type=F32)
            state[h] = st * jnp.exp(g_last) + upd
            on = o * lax.rsqrt(jnp.mean(o * o, axis=-1, keepdims=True) + EPS) * onw_ref[...]
            zz = z_ref[0, rows, cs]
            y_ref[0, rows, cs] = (on * _silu(zz)).astype(y_ref.dtype)

    xbuf[0:HALO, :] = xbuf[t:t + HALO, :]


def _deltanet(p3, ba3, w_short, hp, onw, *, heads, dh, col0):
    b, s, _ = p3.shape
    w = heads * dh
    taps = w_short.shape[0]
    assert taps - 1 <= HALO and s % DN_TILE == 0 and col0 % w == 0
    cb = col0 // w
    t = DN_TILE
    return pl.pallas_call(
        functools.partial(_dn_kernel, heads=heads, dh=dh, taps=taps),
        grid=(b, s // t),
        in_specs=[
            pl.BlockSpec((1, t, w), lambda i, j: (i, j, cb)),
            pl.BlockSpec((1, t, w), lambda i, j: (i, j, cb + 1)),
            pl.BlockSpec((1, t, w), lambda i, j: (i, j, cb + 2)),
            pl.BlockSpec((1, t, w), lambda i, j: (i, j, cb + 3)),
            pl.BlockSpec((1, t, LANES), lambda i, j: (i, j, 0)),
            pl.BlockSpec((taps, 3 * w), lambda i, j: (0, 0)),
            pl.BlockSpec((2, LANES), lambda i, j: (0, 0)),
            pl.BlockSpec((1, dh), lambda i, j: (0, 0)),
        ],
        out_specs=pl.BlockSpec((1, t, w), lambda i, j: (i, j, 0)),
        out_shape=jax.ShapeDtypeStruct((b, s, w), BF16),
        scratch_shapes=[pltpu.VMEM((t + HALO, 3 * w), F32), pltpu.VMEM((heads, dh, dh), F32)],
        compiler_params=pltpu.CompilerParams(
            dimension_semantics=("parallel", "arbitrary"), vmem_limit_bytes=VMEM_LIMIT),
        name="deltanet",
    )(p3, p3, p3, p3, ba3, w_short, hp, onw)


def _outproj_kernel(x_ref, mods_ref, yc_ref, yd_ref, wc_ref, wd_ref, o_ref, *, mod_idx):
    y = jnp.dot(yc_ref[...], wc_ref[...], preferred_element_type=F32)
    y = y + jnp.dot(yd_ref[...], wd_ref[...], preferred_element_type=F32)
    o_ref[...] = x_ref[...] + mods_ref[0, mod_idx:mod_idx + 1, :] * y


def _outproj(x2, mods, yc, yd, w_c, w_d, *, mod_idx, seq, tm=512):
    m, d = x2.shape
    tm = _tile(seq, tm)
    tiles_per_batch = seq // tm
    return pl.pallas_call(
        functools.partial(_outproj_kernel, mod_idx=mod_idx),
        grid=(m // tm,),
        in_specs=[
            pl.BlockSpec((tm, d), lambda i: (i, 0)),
            pl.BlockSpec((1, N_MOD, d), lambda i: (i // tiles_per_batch, 0, 0)),
            pl.BlockSpec((tm, yc.shape[1]), lambda i: (i, 0)),
            pl.BlockSpec((tm, yd.shape[1]), lambda i: (i, 0)),
            pl.BlockSpec(w_c.shape, lambda i: (0, 0)),
            pl.BlockSpec(w_d.shape, lambda i: (0, 0)),
        ],
        out_specs=pl.BlockSpec((tm, d), lambda i: (i, 0)),
        out_shape=jax.ShapeDtypeStruct((m, d), F32),
        compiler_params=pltpu.CompilerParams(
            dimension_semantics=("parallel",), vmem_limit_bytes=VMEM_LIMIT),
        name="out_proj",
    )(x2, mods, yc, yd, w_c, w_d)


def kernel(x, c, w_ada, b_ada, ffn1_norm, ffn1_wg, ffn1_wu, ffn1_wd, mix_norm, w_in, w_dw, b_dw,
           conv_ln_w, conv_ln_b, w_short, a_log, dt_bias, dn_norm_w, w_out, ffn2_norm, ffn2_wg,
           ffn2_wu, ffn2_wd, final_norm):
    b, s, d = x.shape
    depth = w_ada.shape[0]
    heads = a_log.shape[1]
    dh = dn_norm_w.shape[1]
    cw = w_dw.shape[2]
    dnw = heads * dh
    n_main = 2 * cw + 4 * dnw
    assert w_in.shape[2] == n_main + 2 * heads and 2 * heads <= LANES and cw == dnw

    x2 = x.reshape(b * s, d)
    c_pad = jnp.pad(c, ((0, (-b) % HALO), (0, 0)))
    row = lambda v: v.reshape(1, -1)
    for l in range(depth):
        mods = _mods(c_pad, w_ada[l], row(b_ada[l]))[:b].reshape(b, N_MOD, d)
        last = l == depth - 1

        x2 = _ffn(x2, mods, row(ffn1_norm[l]), ffn1_wg[l].astype(BF16), ffn1_wu[l].astype(BF16),
                  ffn1_wd[l].astype(BF16), None, mod_base=0, seq=s)

        w_main = w_in[l][:, :n_main].astype(BF16)
        w_small = jnp.pad(w_in[l][:, n_main:], ((0, 0), (0, LANES - 2 * heads))).astype(BF16)
        p, ba = _inproj(x2, mods, row(mix_norm[l]), w_main, w_small, mod_base=3, seq=s)
        p3 = p.reshape(b, s, n_main)
        ba3 = ba.reshape(b, s, LANES)

        y_conv = _conv_group(p3, w_dw[l], row(b_dw[l]), row(conv_ln_w[l]), row(conv_ln_b[l]))

        hp = jnp.zeros((2, LANES), F32)
        hp = hp.at[0, heads:2 * heads].set(a_log[l]).at[1, heads:2 * heads].set(dt_bias[l])
        y_dn = _deltanet(p3, ba3, w_short[l], hp, row(dn_norm_w[l]), heads=heads, dh=dh, col0=2 * cw)

        wo = w_out[l].astype(BF16)
        x2 = _outproj(x2, mods, y_conv.reshape(b * s, cw), y_dn.reshape(b * s, dnw),
                      wo[:cw], wo[cw:], mod_idx=5, seq=s)

        x2 = _ffn(x2, mods, row(ffn2_norm[l]), ffn2_wg[l].astype(BF16), ffn2_wu[l].astype(BF16),
                  ffn2_wd[l].astype(BF16), row(final_norm) if last else None, mod_base=6, seq=s)
    if depth == 0:
        raise ValueError("depth must be positive")
    return x2.reshape(b, s, d)
```

```python
import functools

import jax
import jax.numpy as jnp
from jax import lax
from jax.experimental import pallas as pl
from jax.experimental.pallas import tpu as pltpu

F32 = jnp.float32
BF16 = jnp.bfloat16

EPS = 1e-6
N_MOD = 9
DN_CHUNK = 64
DN_TILE = 2 * DN_CHUNK
INV_BASE = 8
LANES = 128
HALO = 8
CONV_HALO = 32
VMEM_LIMIT = 56 * 1024 * 1024


def _tile(n, pref):
    t = min(n, pref)
    assert n % t == 0, (n, pref)
    return t


def _sigmoid(v):
    return jax.nn.sigmoid(v)


def _silu(v):
    return v * _sigmoid(v)


def _bdot(a, b):
    return jnp.dot(a.astype(BF16), b.astype(BF16), preferred_element_type=F32)


def _norm_mod(x, nw, shift, scale):
    ms = jnp.mean(x * x, axis=-1, keepdims=True)
    y = x * lax.rsqrt(ms + EPS) * nw
    return y * (1.0 + scale) + shift


def _mods_kernel(c_ref, w_ref, b_ref, o_ref):
    s = _silu(c_ref[...])
    o_ref[...] = _bdot(s, w_ref[...]) + b_ref[...]


def _mods(c_pad, w_ada, b_ada):
    rows, d = c_pad.shape
    n = w_ada.shape[1]
    tn = _tile(d, 1024)
    assert n % tn == 0
    return pl.pallas_call(
        _mods_kernel,
        grid=(n // tn,),
        in_specs=[
            pl.BlockSpec((rows, d), lambda j: (0, 0)),
            pl.BlockSpec((d, tn), lambda j: (0, j)),
            pl.BlockSpec((1, tn), lambda j: (0, j)),
        ],
        out_specs=pl.BlockSpec((rows, tn), lambda j: (0, j)),
        out_shape=jax.ShapeDtypeStruct((rows, n), F32),
        compiler_params=pltpu.CompilerParams(
            dimension_semantics=("parallel",), vmem_limit_bytes=VMEM_LIMIT),
        name="mods",
    )(c_pad, w_ada, b_ada)


def _ffn_kernel(x_ref, mods_ref, nw_ref, wg_ref, wu_ref, wd_ref, *rest, mod_base, final):
    if final:
        fnw_ref, o_ref, h_scr, acc_scr = rest
    else:
        o_ref, h_scr, acc_scr = rest
    f = pl.program_id(1)

    @pl.when(f == 0)
    def _():
        shift = mods_ref[0, mod_base:mod_base + 1, :]
        scale = mods_ref[0, mod_base + 1:mod_base + 2, :]
        h_scr[...] = _norm_mod(x_ref[...], nw_ref[...], shift, scale).astype(BF16)
        acc_scr[...] = jnp.zeros_like(acc_scr)

    h = h_scr[...]
    g = jnp.dot(h, wg_ref[...], preferred_element_type=F32)
    u = jnp.dot(h, wu_ref[...], preferred_element_type=F32)
    a = (_silu(g) * u).astype(BF16)
    acc_scr[...] += jnp.dot(a, wd_ref[...], preferred_element_type=F32)

    @pl.when(f == pl.num_programs(1) - 1)
    def _():
        gate = mods_ref[0, mod_base + 2:mod_base + 3, :]
        y = x_ref[...] + 0.5 * gate * acc_scr[...]
        if final:
            ms = jnp.mean(y * y, axis=-1, keepdims=True)
            y = y * lax.rsqrt(ms + EPS) * fnw_ref[...]
        o_ref[...] = y


def _ffn(x2, mods, nw, wg, wu, wd, fnw, *, mod_base, seq, tm=512, tf=512):
    m, d = x2.shape
    dff = wg.shape[1]
    tm = _tile(seq, tm)
    tf = _tile(dff, tf)
    tiles_per_batch = seq // tm
    final = fnw is not None
    in_specs = [
        pl.BlockSpec((tm, d), lambda i, f: (i, 0)),
        pl.BlockSpec((1, N_MOD, d), lambda i, f: (i // tiles_per_batch, 0, 0)),
        pl.BlockSpec((1, d), lambda i, f: (0, 0)),
        pl.BlockSpec((d, tf), lambda i, f: (0, f)),
        pl.BlockSpec((d, tf), lambda i, f: (0, f)),
        pl.BlockSpec((tf, d), lambda i, f: (f, 0)),
    ]
    args = [x2, mods, nw, wg, wu, wd]
    if final:
        in_specs.append(pl.BlockSpec((1, d), lambda i, f: (0, 0)))
        args.append(fnw)
    return pl.pallas_call(
        functools.partial(_ffn_kernel, mod_base=mod_base, final=final),
        grid=(m // tm, dff // tf),
        in_specs=in_specs,
        out_specs=pl.BlockSpec((tm, d), lambda i, f: (i, 0)),
        out_shape=jax.ShapeDtypeStruct((m, d), F32),
        scratch_shapes=[pltpu.VMEM((tm, d), BF16), pltpu.VMEM((tm, d), F32)],
        compiler_params=pltpu.CompilerParams(
            dimension_semantics=("parallel", "arbitrary"), vmem_limit_bytes=VMEM_LIMIT),
        name="ffn_final" if final else "ffn",
    )(*args)


def _inproj_kernel(x_ref, mods_ref, nw_ref, w_ref, ws_ref, p_ref, ba_ref, h_scr, *, mod_base):
    n = pl.program_id(1)

    @pl.when(n == 0)
    def _():
        shift = mods_ref[0, mod_base:mod_base + 1, :]
        scale = mods_ref[0, mod_base + 1:mod_base + 2, :]
        h = _norm_mod(x_ref[...], nw_ref[...], shift, scale).astype(BF16)
        h_scr[...] = h
        ba_ref[...] = jnp.dot(h, ws_ref[...], preferred_element_type=F32)

    p_ref[...] = jnp.dot(h_scr[...], w_ref[...], preferred_element_type=F32)


def _inproj(x2, mods, nw, w_main, w_small, *, mod_base, seq, tm=1024, tn=1024):
    m, d = x2.shape
    ncols = w_main.shape[1]
    tm = _tile(seq, tm)
    tn = min(tn, ncols // 6)
    assert ncols % tn == 0
    tiles_per_batch = seq // tm
    return pl.pallas_call(
        functools.partial(_inproj_kernel, mod_base=mod_base),
        grid=(m // tm, ncols // tn),
        in_specs=[
            pl.BlockSpec((tm, d), lambda i, n: (i, 0)),
            pl.BlockSpec((1, N_MOD, d), lambda i, n: (i // tiles_per_batch, 0, 0)),
            pl.BlockSpec((1, d), lambda i, n: (0, 0)),
            pl.BlockSpec((d, tn), lambda i, n: (0, n)),
            pl.BlockSpec((d, LANES), lambda i, n: (0, 0)),
        ],
        out_specs=[
            pl.BlockSpec((tm, tn), lambda i, n: (i, n)),
            pl.BlockSpec((tm, LANES), lambda i, n: (i, 0)),
        ],
        out_shape=[
            jax.ShapeDtypeStruct((m, ncols), F32),
            jax.ShapeDtypeStruct((m, LANES), F32),
        ],
        scratch_shapes=[pltpu.VMEM((tm, d), BF16)],
        compiler_params=pltpu.CompilerParams(
            dimension_semantics=("parallel", "arbitrary"), vmem_limit_bytes=VMEM_LIMIT),
        name="in_proj",
    )(x2, mods, nw, w_main, w_small)


def _conv_kernel(ca_ref, cg_ref, wdw_ref, bdw_ref, lnw_ref, lnb_ref, y_ref, buf, shf, *, taps, tc, rb):
    s = pl.program_id(1)

    @pl.when(s == 0)
    def _():
        buf[0:CONV_HALO, :] = jnp.zeros((CONV_HALO, buf.shape[1]), F32)

    buf[CONV_HALO:CONV_HALO + tc, :] = ca_ref[0] * _sigmoid(cg_ref[0])
    first = CONV_HALO - (taps - 1)
    nshift = shf.shape[1]
    for b in range(1, HALO):
        shf[b - 1] = buf[b:b + nshift, :]

    def window(off, r):
        a, b = divmod(off, HALO)
        if b == 0:
            return buf[HALO * a + r:HALO * a + r + rb, :]
        return shf[b - 1, HALO * a + r:HALO * a + r + rb, :]

    for r in range(0, tc, rb):
        acc = wdw_ref[0:1, :] * window(first, r)
        for j in range(1, taps):
            acc = acc + wdw_ref[j:j + 1, :] * window(first + j, r)
        hh = acc + bdw_ref[...]
        mu = jnp.mean(hh, axis=-1, keepdims=True)
        cen = hh - mu
        var = jnp.mean(cen * cen, axis=-1, keepdims=True)
        hn = cen * lax.rsqrt(var + EPS) * lnw_ref[...] + lnb_ref[...]
        y_ref[0, r:r + rb, :] = _silu(hn).astype(y_ref.dtype)
    buf[0:CONV_HALO, :] = buf[tc:tc + CONV_HALO, :]


def _conv_group(p3, w_dw, b_dw, ln_w, ln_b, *, tc=256, rb=32):
    b, s, _ = p3.shape
    taps, cw = w_dw.shape
    assert taps - 1 <= CONV_HALO
    tc = _tile(s, tc)
    rb = _tile(tc, rb)
    return pl.pallas_call(
        functools.partial(_conv_kernel, taps=taps, tc=tc, rb=rb),
        grid=(b, s // tc),
        in_specs=[
            pl.BlockSpec((1, tc, cw), lambda i, j: (i, j, 0)),
            pl.BlockSpec((1, tc, cw), lambda i, j: (i, j, 1)),
            pl.BlockSpec((taps, cw), lambda i, j: (0, 0)),
            pl.BlockSpec((1, cw), lambda i, j: (0, 0)),
            pl.BlockSpec((1, cw), lambda i, j: (0, 0)),
            pl.BlockSpec((1, cw), lambda i, j: (0, 0)),
        ],
        out_specs=pl.BlockSpec((1, tc, cw), lambda i, j: (i, j, 0)),
        out_shape=jax.ShapeDtypeStruct((b, s, cw), BF16),
        scratch_shapes=[pltpu.VMEM((tc + CONV_HALO, cw), F32),
                        pltpu.VMEM((HALO - 1, tc + CONV_HALO - HALO, cw), F32)],
        compiler_params=pltpu.CompilerParams(
            dimension_semantics=("parallel", "arbitrary"), vmem_limit_bytes=VMEM_LIMIT),
        name="conv_group",
    )(p3, p3, w_dw, b_dw, ln_w, ln_b)


def _dn_kernel(q_ref, k_ref, v_ref, z_ref, ba_ref, wsh_ref, hp_ref, onw_ref, y_ref,
               xbuf, state, *, heads, dh, taps):
    t, c = DN_TILE, DN_CHUNK
    w = heads * dh
    s = pl.program_id(1)

    @pl.when(s == 0)
    def _():
        xbuf[0:HALO, :] = jnp.zeros((HALO, 3 * w), F32)
        state[...] = jnp.zeros_like(state)

    xbuf[HALO:HALO + t, 0:w] = q_ref[0]
    xbuf[HALO:HALO + t, w:2 * w] = k_ref[0]
    xbuf[HALO:HALO + t, 2 * w:3 * w] = v_ref[0]
    first = HALO - (taps - 1)

    ba = ba_ref[0]
    sig = _sigmoid(ba)
    xx = ba + hp_ref[1:2, :]
    softplus = jnp.maximum(xx, 0.0) + jnp.log1p(jnp.exp(-jnp.abs(xx)))
    gg = -jnp.exp(hp_ref[0:1, :]) * softplus

    row = lax.broadcasted_iota(jnp.int32, (t, t), 0)
    col = lax.broadcasted_iota(jnp.int32, (t, t), 1)
    same = (row // c) == (col // c)
    tril = same & (row >= col)
    strict = same & (row > col)
    eye = (row == col).astype(F32)
    blk = lambda n: (row // n) == (col // n)
    base_mask = blk(INV_BASE)
    merge_masks = []
    n = INV_BASE
    while n < c:
        merge_masks.append(blk(2 * n) & jnp.logical_not(blk(n)))
        n *= 2
    gcum = jnp.dot(tril.astype(F32), gg, preferred_element_type=F32,
                   precision=lax.Precision.HIGHEST)
    gcum_t = gcum.T

    hs = range(heads)
    cols = [slice(h * dh, (h + 1) * dh) for h in hs]

    def short_conv(base, h):
        lo = base + h * dh
        acc = wsh_ref[0:1, lo:lo + dh] * xbuf[first:first + t, lo:lo + dh]
        for j in range(1, taps):
            acc = acc + wsh_ref[j:j + 1, lo:lo + dh] * xbuf[first + j:first + j + t, lo:lo + dh]
        return _silu(acc)

    def l2n(v):
        return v * lax.rsqrt(jnp.sum(v * v, axis=-1, keepdims=True) + EPS)

    q = [l2n(short_conv(0, h)) * (dh ** -0.5) for h in hs]
    k = [l2n(short_conv(w, h)) for h in hs]
    v = [short_conv(2 * w, h) for h in hs]
    beta = [sig[:, h:h + 1] for h in hs]
    g_col = [gcum[:, heads + h:heads + h + 1] for h in hs]
    e_g = [jnp.exp(g) for g in g_col]
    decay = [jnp.exp(jnp.where(tril, g_col[h] - gcum_t[heads + h:heads + h + 1, :], -jnp.inf))
             for h in hs]
    kb = [k[h] * beta[h] for h in hs]
    aq = [lax.dot_general(jnp.concatenate([kb[h], q[h]], axis=0).astype(BF16), k[h].astype(BF16),
                          (((1,), (1,)), ((), ())), preferred_element_type=F32) for h in hs]
    a_intra = [aq[h][t:] * decay[h] for h in hs]
    lmat = [aq[h][:t] * jnp.where(strict, decay[h], 0.0) for h in hs]

    xp = [-jnp.where(base_mask, lmat[h], 0.0) for h in hs]
    rr = [eye + xp[h] for h in hs]
    xp = [_bdot(xp[h], xp[h]) for h in hs]
    for _ in range(INV_BASE.bit_length() - 3):
        pr = [_bdot(jnp.concatenate([rr[h], xp[h]], axis=0), xp[h]) for h in hs]
        rr = [rr[h] + pr[h][:t] for h in hs]
        xp = [pr[h][t:] for h in hs]
    pr = [_bdot(rr[h], xp[h]) for h in hs]
    rr = [rr[h] + pr[h] for h in hs]
    for off_mask in merge_masks:
        pr = [_bdot(rr[h], jnp.where(off_mask, lmat[h], 0.0)) for h in hs]
        pr = [_bdot(pr[h], rr[h]) for h in hs]
        rr = [rr[h] - pr[h] for h in hs]

    sol = [_bdot(rr[h], jnp.concatenate([v[h] * beta[h], kb[h] * e_g[h]], axis=1)) for h in hs]
    qg = [q[h] * e_g[h] for h in hs]

    st = [state[h] for h in hs]
    for ci in range(t // c):
        rows = slice(ci * c, (ci + 1) * c)
        wq = [_bdot(jnp.concatenate([sol[h][rows, dh:], qg[h][rows]], axis=0), st[h]) for h in hs]
        v_new = [sol[h][rows, :dh] - wq[h][:c] for h in hs]
        g_last = [g_col[h][ci * c + c - 1:ci * c + c, :] for h in hs]
        kd = [k[h][rows] * jnp.exp(g_last[h] - g_col[h][rows]) for h in hs]
        upd = [lax.dot_general(kd[h].astype(BF16), v_new[h].astype(BF16),
                               (((0,), (0,)), ((), ())), preferred_element_type=F32) for h in hs]
        st = [st[h] * jnp.exp(g_last[h]) + upd[h] for h in hs]
        zeros = jnp.zeros((c, dh), F32)
        for h in hs:
            parts = [zeros] * (t // c)
            parts[ci] = v_new[h]
            v_pad = jnp.concatenate(parts, axis=0)
            o = wq[h][c:] + _bdot(a_intra[h][rows], v_pad)
            on = o * lax.rsqrt(jnp.mean(o * o, axis=-1, keepdims=True) + EPS) * onw_ref[...]
            zz = z_ref[0, rows, cols[h]]
            y_ref[0, rows, cols[h]] = (on * _silu(zz)).astype(y_ref.dtype)
    for h in hs:
        state[h] = st[h]

    xbuf[0:HALO, :] = xbuf[t:t + HALO, :]


def _deltanet(p3, ba3, w_short, hp, onw, *, heads, dh, col0):
    b, s, _ = p3.shape
    w = heads * dh
    taps = w_short.shape[0]
    assert taps - 1 <= HALO and s % DN_TILE == 0 and col0 % w == 0
    cb = col0 // w
    t = DN_TILE
    return pl.pallas_call(
        functools.partial(_dn_kernel, heads=heads, dh=dh, taps=taps),
        grid=(b, s // t),
        in_specs=[
            pl.BlockSpec((1, t, w), lambda i, j: (i, j, cb)),
            pl.BlockSpec((1, t, w), lambda i, j: (i, j, cb + 1)),
            pl.BlockSpec((1, t, w), lambda i, j: (i, j, cb + 2)),
            pl.BlockSpec((1, t, w), lambda i, j: (i, j, cb + 3)),
            pl.BlockSpec((1, t, LANES), lambda i, j: (i, j, 0)),
            pl.BlockSpec((taps, 3 * w), lambda i, j: (0, 0)),
            pl.BlockSpec((2, LANES), lambda i, j: (0, 0)),
            pl.BlockSpec((1, dh), lambda i, j: (0, 0)),
        ],
        out_specs=pl.BlockSpec((1, t, w), lambda i, j: (i, j, 0)),
        out_shape=jax.ShapeDtypeStruct((b, s, w), BF16),
        scratch_shapes=[pltpu.VMEM((t + HALO, 3 * w), F32), pltpu.VMEM((heads, dh, dh), F32)],
        compiler_params=pltpu.CompilerParams(
            dimension_semantics=("parallel", "arbitrary"), vmem_limit_bytes=VMEM_LIMIT),
        name="deltanet",
    )(p3, p3, p3, p3, ba3, w_short, hp, onw)


def _outproj_kernel(x_ref, mods_ref, yc_ref, yd_ref, wc_ref, wd_ref, o_ref, *, mod_idx):
    y = jnp.dot(yc_ref[...], wc_ref[...], preferred_element_type=F32)
    y = y + jnp.dot(yd_ref[...], wd_ref[...], preferred_element_type=F32)
    o_ref[...] = x_ref[...] + mods_ref[0, mod_idx:mod_idx + 1, :] * y


def _outproj(x2, mods, yc, yd, w_c, w_d, *, mod_idx, seq, tm=512):
    m, d = x2.shape
    tm = _tile(seq, tm)
    tiles_per_batch = seq // tm
    return pl.pallas_call(
        functools.partial(_outproj_kernel, mod_idx=mod_idx),
        grid=(m // tm,),
        in_specs=[
            pl.BlockSpec((tm, d), lambda i: (i, 0)),
            pl.BlockSpec((1, N_MOD, d), lambda i: (i // tiles_per_batch, 0, 0)),
            pl.BlockSpec((tm, yc.shape[1]), lambda i: (i, 0)),
            pl.BlockSpec((tm, yd.shape[1]), lambda i: (i, 0)),
            pl.BlockSpec(w_c.shape, lambda i: (0, 0)),
            pl.BlockSpec(w_d.shape, lambda i: (0, 0)),
        ],
        out_specs=pl.BlockSpec((tm, d), lambda i: (i, 0)),
        out_shape=jax.ShapeDtypeStruct((m, d), F32),
        compiler_params=pltpu.CompilerParams(
            dimension_semantics=("parallel",), vmem_limit_bytes=VMEM_LIMIT),
        name="out_proj",
    )(x2, mods, yc, yd, w_c, w_d)


def kernel(x, c, w_ada, b_ada, ffn1_norm, ffn1_wg, ffn1_wu, ffn1_wd, mix_norm, w_in, w_dw, b_dw,
           conv_ln_w, conv_ln_b, w_short, a_log, dt_bias, dn_norm_w, w_out, ffn2_norm, ffn2_wg,
           ffn2_wu, ffn2_wd, final_norm):
    b, s, d = x.shape
    depth = w_ada.shape[0]
    heads = a_log.shape[1]
    dh = dn_norm_w.shape[1]
    cw = w_dw.shape[2]
    dnw = heads * dh
    n_main = 2 * cw + 4 * dnw
    assert w_in.shape[2] == n_main + 2 * heads and 2 * heads <= LANES and cw == dnw

    x2 = x.reshape(b * s, d)
    c_pad = jnp.pad(c, ((0, (-b) % HALO), (0, 0)))
    row = lambda v: v.reshape(1, -1)
    for l in range(depth):
        mods = _mods(c_pad, w_ada[l], row(b_ada[l]))[:b].reshape(b, N_MOD, d)
        last = l == depth - 1

        x2 = _ffn(x2, mods, row(ffn1_norm[l]), ffn1_wg[l].astype(BF16), ffn1_wu[l].astype(BF16),
                  ffn1_wd[l].astype(BF16), None, mod_base=0, seq=s)

        w_main = w_in[l][:, :n_main].astype(BF16)
        w_small = jnp.pad(w_in[l][:, n_main:], ((0, 0), (0, LANES - 2 * heads))).astype(BF16)
        p, ba = _inproj(x2, mods, row(mix_norm[l]), w_main, w_small, mod_base=3, seq=s)
        p3 = p.reshape(b, s, n_main)
        ba3 = ba.reshape(b, s, LANES)

        y_conv = _conv_group(p3, w_dw[l], row(b_dw[l]), row(conv_ln_w[l]), row(conv_ln_b[l]))

        hp = jnp.zeros((2, LANES), F32)
        hp = hp.at[0, heads:2 * heads].set(a_log[l]).at[1, heads:2 * heads].set(dt_bias[l])
        y_dn = _deltanet(p3, ba3, w_short[l], hp, row(dn_norm_w[l]), heads=heads, dh=dh, col0=2 * cw)

        wo = w_out[l].astype(BF16)
        x2 = _outproj(x2, mods, y_conv.reshape(b * s, cw), y_dn.reshape(b * s, dnw),
                      wo[:cw], wo[cw:], mod_idx=5, seq=s)

        x2 = _ffn(x2, mods, row(ffn2_norm[l]), ffn2_wg[l].astype(BF16), ffn2_wu[l].astype(BF16),
                  ffn2_wd[l].astype(BF16), row(final_norm) if last else None, mod_base=6, seq=s)
    if depth == 0:
        raise ValueError("depth must be positive")
    return x2.reshape(b, s, d)
```

```python
import functools

import jax
import jax.numpy as jnp
from jax import lax
from jax.experimental import pallas as pl
from jax.experimental.pallas import tpu as pltpu

F32 = jnp.float32
BF16 = jnp.bfloat16

EPS = 1e-6
N_MOD = 9
DN_CHUNK = 64
DN_TILE = 2 * DN_CHUNK
INV_BASE = 8
LANES = 128
HALO = 8
ROW_CHUNK = 32
ROW_UNROLL = 4
CONV_MATMUL_ROWS = 256
CONV_HALO = 32
VMEM_LIMIT = 60 * 1024 * 1024


def _tile(n, pref):
    t = min(n, pref)
    assert n % t == 0, (n, pref)
    return t


def _sigmoid(v):
    return jax.nn.sigmoid(v)


def _silu(v):
    return v * _sigmoid(v)


def _bdot(a, b):
    return jnp.dot(a.astype(BF16), b.astype(BF16), preferred_element_type=F32)


def _row_chunks(n_rows, fn):
    rc = min(ROW_CHUNK, n_rows)
    assert n_rows % rc == 0

    def body(i, carry):
        fn(pl.ds(pl.multiple_of(i * rc, rc), rc))
        return carry

    trips = n_rows // rc
    lax.fori_loop(0, trips, body, 0, unroll=min(ROW_UNROLL, trips))


def _norm_mod_store(x_ref, h_ref, nw, shift, scale):
    def chunk(rows):
        x = x_ref[rows, :]
        ms = jnp.mean(x * x, axis=-1, keepdims=True)
        y = x * lax.rsqrt(ms + EPS) * nw
        h_ref[rows, :] = (y * (1.0 + scale) + shift).astype(h_ref.dtype)

    _row_chunks(x_ref.shape[0], chunk)


def _mods_kernel(c_ref, w_ref, b_ref, o_ref):
    s = _silu(c_ref[...])
    o_ref[...] = _bdot(s, w_ref[...]) + b_ref[...]


def _mods(c_pad, w_ada, b_ada):
    rows, d = c_pad.shape
    n = w_ada.shape[1]
    tn = _tile(d, 1024)
    assert n % tn == 0
    return pl.pallas_call(
        _mods_kernel,
        grid=(n // tn,),
        in_specs=[
            pl.BlockSpec((rows, d), lambda j: (0, 0)),
            pl.BlockSpec((d, tn), lambda j: (0, j)),
            pl.BlockSpec((1, tn), lambda j: (0, j)),
        ],
        out_specs=pl.BlockSpec((rows, tn), lambda j: (0, j)),
        out_shape=jax.ShapeDtypeStruct((rows, n), F32),
        compiler_params=pltpu.CompilerParams(
            dimension_semantics=("parallel",), vmem_limit_bytes=VMEM_LIMIT),
        name="mods",
    )(c_pad, w_ada, b_ada)


def _ffn_kernel(x_ref, mods_ref, nw_ref, wg_ref, wu_ref, wd_ref, *rest, mod_base, final):
    if final:
        fnw_ref, o_ref, h_scr = rest
    else:
        o_ref, h_scr = rest
    f = pl.program_id(1)

    @pl.when(f == 0)
    def _():
        shift = mods_ref[0, mod_base:mod_base + 1, :]
        scale = mods_ref[0, mod_base + 1:mod_base + 2, :]
        _norm_mod_store(x_ref, h_scr, nw_ref[...], shift, scale)
        o_ref[...] = jnp.zeros_like(o_ref)

    h = h_scr[...]
    g = jnp.dot(h, wg_ref[...], preferred_element_type=F32)
    u = jnp.dot(h, wu_ref[...], preferred_element_type=F32)
    a = (_silu(g) * u).astype(BF16)
    o_ref[...] += jnp.dot(a, wd_ref[...], preferred_element_type=F32)

    @pl.when(f == pl.num_programs(1) - 1)
    def _():
        gate = mods_ref[0, mod_base + 2:mod_base + 3, :]

        def chunk(rows):
            y = x_ref[rows, :] + 0.5 * gate * o_ref[rows, :]
            if final:
                ms = jnp.mean(y * y, axis=-1, keepdims=True)
                y = y * lax.rsqrt(ms + EPS) * fnw_ref[...]
            o_ref[rows, :] = y

        _row_chunks(o_ref.shape[0], chunk)


def _ffn(x2, mods, nw, wg, wu, wd, fnw, *, mod_base, seq, tm=1024, tf=512):
    m, d = x2.shape
    dff = wg.shape[1]
    tm = _tile(seq, tm)
    tf = _tile(dff, tf)
    tiles_per_batch = seq // tm
    final = fnw is not None
    in_specs = [
        pl.BlockSpec((tm, d), lambda i, f: (i, 0)),
        pl.BlockSpec((1, N_MOD, d), lambda i, f: (i // tiles_per_batch, 0, 0)),
        pl.BlockSpec((1, d), lambda i, f: (0, 0)),
        pl.BlockSpec((d, tf), lambda i, f: (0, f)),
        pl.BlockSpec((d, tf), lambda i, f: (0, f)),
        pl.BlockSpec((tf, d), lambda i, f: (f, 0)),
    ]
    args = [x2, mods, nw, wg, wu, wd]
    if final:
        in_specs.append(pl.BlockSpec((1, d), lambda i, f: (0, 0)))
        args.append(fnw)
    return pl.pallas_call(
        functools.partial(_ffn_kernel, mod_base=mod_base, final=final),
        grid=(m // tm, dff // tf),
        in_specs=in_specs,
        out_specs=pl.BlockSpec((tm, d), lambda i, f: (i, 0)),
        out_shape=jax.ShapeDtypeStruct((m, d), F32),
        scratch_shapes=[pltpu.VMEM((tm, d), BF16)],
        compiler_params=pltpu.CompilerParams(
            dimension_semantics=("parallel", "arbitrary"), vmem_limit_bytes=VMEM_LIMIT),
        name="ffn_final" if final else "ffn",
    )(*args)


def _inproj_kernel(x_ref, mods_ref, nw_ref, w_ref, ws_ref, wsh_ref, p_ref, ba_ref, h_scr, cbuf,
                   carry, *, mod_base, tiles_per_batch, conv0, dh, taps, rc, mb):
    i = pl.program_id(0)
    n = pl.program_id(1)
    tm, tn = p_ref.shape

    @pl.when(n == 0)
    def _():
        shift = mods_ref[0, mod_base:mod_base + 1, :]
        scale = mods_ref[0, mod_base + 1:mod_base + 2, :]
        _norm_mod_store(x_ref, h_scr, nw_ref[...], shift, scale)
        ba_ref[...] = jnp.dot(h_scr[...], ws_ref[...], preferred_element_type=F32)

    @pl.when((n < conv0) | (n >= conv0 + 3))
    def _():
        p_ref[...] = jnp.dot(h_scr[...], w_ref[...], preferred_element_type=F32)

    first = HALO - (taps - 1)
    for kk in range(3):
        @pl.when(n == conv0 + kk)
        def _(kk=kk):
            @pl.when(i % tiles_per_batch == 0)
            def _():
                carry[kk] = jnp.zeros((HALO, tn), F32)

            def conv_rows(r):
                acc = wsh_ref[0:1, :] * cbuf[first + r:first + r + rc, :]
                for j in range(1, taps):
                    acc = acc + wsh_ref[j:j + 1, :] * cbuf[first + r + j:first + r + j + rc, :]
                y = _silu(acc)
                if kk == 2:
                    p_ref[r:r + rc, :] = y
                    return
                for h in range(tn // dh):
                    yh = y[:, h * dh:(h + 1) * dh]
                    yh = yh * lax.rsqrt(jnp.sum(yh * yh, axis=-1, keepdims=True) + EPS)
                    if kk == 0:
                        yh = yh * (dh ** -0.5)
                    p_ref[r:r + rc, h * dh:(h + 1) * dh] = yh

            cbuf[0:HALO, :] = carry[kk]
            for b in range(tm // mb + 1):
                if b < tm // mb:
                    cbuf[HALO + b * mb:HALO + (b + 1) * mb, :] = jnp.dot(
                        h_scr[b * mb:(b + 1) * mb, :], w_ref[...], preferred_element_type=F32)
                if b > 0:
                    for r in range((b - 1) * mb, b * mb, rc):
                        conv_rows(r)
            carry[kk] = cbuf[tm:tm + HALO, :]


def _inproj(x2, mods, nw, w_main, w_small, w_short, *, mod_base, seq, dh, conv0, tm=1024, tn=1024):
    m, d = x2.shape
    ncols = w_main.shape[1]
    taps = w_short.shape[0]
    tm = _tile(seq, tm)
    tn = min(tn, ncols // 6)
    assert ncols % tn == 0 and w_short.shape[1] == 3 * tn and tn % dh == 0 and taps - 1 <= HALO
    tiles_per_batch = seq // tm
    return pl.pallas_call(
        functools.partial(_inproj_kernel, mod_base=mod_base, tiles_per_batch=tiles_per_batch,
                          conv0=conv0, dh=dh, taps=taps, rc=min(ROW_CHUNK, tm),
                          mb=_tile(tm, CONV_MATMUL_ROWS)),
        grid=(m // tm, ncols // tn),
        in_specs=[
            pl.BlockSpec((tm, d), lambda i, n: (i, 0)),
            pl.BlockSpec((1, N_MOD, d), lambda i, n: (i // tiles_per_batch, 0, 0)),
            pl.BlockSpec((1, d), lambda i, n: (0, 0)),
            pl.BlockSpec((d, tn), lambda i, n: (0, n)),
            pl.BlockSpec((d, LANES), lambda i, n: (0, 0)),
            pl.BlockSpec((taps, tn), lambda i, n: (0, jnp.clip(n - conv0, 0, 2))),
        ],
        out_specs=[
            pl.BlockSpec((tm, tn), lambda i, n: (i, n)),
            pl.BlockSpec((tm, LANES), lambda i, n: (i, 0)),
        ],
        out_shape=[
            jax.ShapeDtypeStruct((m, ncols), F32),
            jax.ShapeDtypeStruct((m, LANES), F32),
        ],
        scratch_shapes=[pltpu.VMEM((tm, d), BF16), pltpu.VMEM((tm + HALO, tn), F32),
                        pltpu.VMEM((3, HALO, tn), F32)],
        compiler_params=pltpu.CompilerParams(
            dimension_semantics=("arbitrary", "arbitrary"), vmem_limit_bytes=VMEM_LIMIT),
        name="in_proj",
    )(x2, mods, nw, w_main, w_small, w_short)


def _conv_kernel(ca_ref, cg_ref, wdw_ref, bdw_ref, lnw_ref, lnb_ref, y_ref, buf, shf, *, taps, tc, rb):
    s = pl.program_id(1)

    @pl.when(s == 0)
    def _():
        buf[0:CONV_HALO, :] = jnp.zeros((CONV_HALO, buf.shape[1]), F32)

    buf[CONV_HALO:CONV_HALO + tc, :] = ca_ref[0] * _sigmoid(cg_ref[0])
    first = CONV_HALO - (taps - 1)
    nshift = shf.shape[1]
    for b in range(1, HALO):
        shf[b - 1] = buf[b:b + nshift, :]

    def window(off, r):
        a, b = divmod(off, HALO)
        if b == 0:
            return buf[HALO * a + r:HALO * a + r + rb, :]
        return shf[b - 1, HALO * a + r:HALO * a + r + rb, :]

    for r in range(0, tc, rb):
        acc = wdw_ref[0:1, :] * window(first, r)
        for j in range(1, taps):
            acc = acc + wdw_ref[j:j + 1, :] * window(first + j, r)
        hh = acc + bdw_ref[...]
        mu = jnp.mean(hh, axis=-1, keepdims=True)
        cen = hh - mu
        var = jnp.mean(cen * cen, axis=-1, keepdims=True)
        hn = cen * lax.rsqrt(var + EPS) * lnw_ref[...] + lnb_ref[...]
        y_ref[0, r:r + rb, :] = _silu(hn).astype(y_ref.dtype)
    buf[0:CONV_HALO, :] = buf[tc:tc + CONV_HALO, :]


def _conv_group(p3, w_dw, b_dw, ln_w, ln_b, *, tc=256, rb=32):
    b, s, _ = p3.shape
    taps, cw = w_dw.shape
    assert taps - 1 <= CONV_HALO
    tc = _tile(s, tc)
    rb = _tile(tc, rb)
    return pl.pallas_call(
        functools.partial(_conv_kernel, taps=taps, tc=tc, rb=rb),
        grid=(b, s // tc),
        in_specs=[
            pl.BlockSpec((1, tc, cw), lambda i, j: (i, j, 0)),
            pl.BlockSpec((1, tc, cw), lambda i, j: (i, j, 1)),
            pl.BlockSpec((taps, cw), lambda i, j: (0, 0)),
            pl.BlockSpec((1, cw), lambda i, j: (0, 0)),
            pl.BlockSpec((1, cw), lambda i, j: (0, 0)),
            pl.BlockSpec((1, cw), lambda i, j: (0, 0)),
        ],
        out_specs=pl.BlockSpec((1, tc, cw), lambda i, j: (i, j, 0)),
        out_shape=jax.ShapeDtypeStruct((b, s, cw), BF16),
        scratch_shapes=[pltpu.VMEM((tc + CONV_HALO, cw), F32),
                        pltpu.VMEM((HALO - 1, tc + CONV_HALO - HALO, cw), F32)],
        compiler_params=pltpu.CompilerParams(
            dimension_semantics=("parallel", "arbitrary"), vmem_limit_bytes=VMEM_LIMIT),
        name="conv_group",
    )(p3, p3, w_dw, b_dw, ln_w, ln_b)


def _dn_kernel(q_ref, k_ref, v_ref, z_ref, ba_ref, hp_ref, onw_ref, y_ref, state, *, heads, dh):
    t, c = DN_TILE, DN_CHUNK
    ng = 2 * heads
    s = pl.program_id(1)

    @pl.when(s == 0)
    def _():
        state[...] = jnp.zeros_like(state)

    row = lax.broadcasted_iota(jnp.int32, (t, t), 0)
    col = lax.broadcasted_iota(jnp.int32, (t, t), 1)
    same = (row // c) == (col // c)
    tril = same & (row >= col)
    strict = same & (row > col)
    triu = same & (row <= col)
    eye = (row == col).astype(F32)
    blk = lambda n: (row // n) == (col // n)
    base_mask = blk(INV_BASE)
    merge_masks = []
    n = INV_BASE
    while n < c:
        merge_masks.append(blk(2 * n) & jnp.logical_not(blk(n)))
        n *= 2

    ba_t = ba_ref[0].T[0:ng, :]
    sig_t = _sigmoid(ba_t)
    xx = ba_t + hp_ref[:, 1:2]
    softplus = jnp.maximum(xx, 0.0) + jnp.log1p(jnp.exp(-jnp.abs(xx)))
    gg_t = -jnp.exp(hp_ref[:, 0:1]) * softplus
    gcum_t = jnp.dot(gg_t, triu.astype(F32), preferred_element_type=F32,
                     precision=lax.Precision.HIGHEST)
    rid = lax.broadcasted_iota(jnp.int32, (ng, t), 0)
    packed = jnp.where(rid < heads, sig_t, gcum_t)
    gate_cols = jnp.concatenate([packed, jnp.zeros((LANES - ng, t), F32)], axis=0).T

    hs = range(heads)
    cols = [slice(h * dh, (h + 1) * dh) for h in hs]

    q = [q_ref[0, :, cols[h]] for h in hs]
    k = [k_ref[0, :, cols[h]] for h in hs]
    v = [v_ref[0, :, cols[h]] for h in hs]
    beta = [gate_cols[:, h:h + 1] for h in hs]
    g_col = [gate_cols[:, heads + h:heads + h + 1] for h in hs]
    e_g = [jnp.exp(g) for g in g_col]
    decay = [jnp.exp(jnp.where(tril, g_col[h] - gcum_t[heads + h:heads + h + 1, :], -jnp.inf))
             for h in hs]
    kb = [k[h] * beta[h] for h in hs]
    aq = [lax.dot_general(jnp.concatenate([kb[h], q[h]], axis=0).astype(BF16), k[h].astype(BF16),
                          (((1,), (1,)), ((), ())), preferred_element_type=F32) for h in hs]
    a_intra = [aq[h][t:] * decay[h] for h in hs]
    lmat = [aq[h][:t] * jnp.where(strict, decay[h], 0.0) for h in hs]

    xp = [-jnp.where(base_mask, lmat[h], 0.0) for h in hs]
    rr = [eye + xp[h] for h in hs]
    xp = [_bdot(xp[h], xp[h]) for h in hs]
    for _ in range(INV_BASE.bit_length() - 3):
        pr = [_bdot(jnp.concatenate([rr[h], xp[h]], axis=0), xp[h]) for h in hs]
        rr = [rr[h] + pr[h][:t] for h in hs]
        xp = [pr[h][t:] for h in hs]
    pr = [_bdot(rr[h], xp[h]) for h in hs]
    rr = [rr[h] + pr[h] for h in hs]
    for off_mask in merge_masks:
        pr = [_bdot(rr[h], jnp.where(off_mask, lmat[h], 0.0)) for h in hs]
        pr = [_bdot(pr[h], rr[h]) for h in hs]
        rr = [rr[h] - pr[h] for h in hs]

    sol = [_bdot(rr[h], jnp.concatenate([v[h] * beta[h], kb[h] * e_g[h]], axis=1)) for h in hs]
    qg = [q[h] * e_g[h] for h in hs]

    st = [state[h] for h in hs]
    for ci in range(t // c):
        rows = slice(ci * c, (ci + 1) * c)
        wq = [_bdot(jnp.concatenate([sol[h][rows, dh:], qg[h][rows]], axis=0), st[h]) for h in hs]
        v_new = [sol[h][rows, :dh] - wq[h][:c] for h in hs]
        g_last = [g_col[h][ci * c + c - 1:ci * c + c, :] for h in hs]
        kd = [k[h][rows] * jnp.exp(g_last[h] - g_col[h][rows]) for h in hs]
        upd = [lax.dot_general(kd[h].astype(BF16), v_new[h].astype(BF16),
                               (((0,), (0,)), ((), ())), preferred_element_type=F32) for h in hs]
        st = [st[h] * jnp.exp(g_last[h]) + upd[h] for h in hs]
        zeros = jnp.zeros((c, dh), F32)
        for h in hs:
            parts = [zeros] * (t // c)
            parts[ci] = v_new[h]
            v_pad = jnp.concatenate(parts, axis=0)
            o = wq[h][c:] + _bdot(a_intra[h][rows], v_pad)
            on = o * lax.rsqrt(jnp.mean(o * o, axis=-1, keepdims=True) + EPS) * onw_ref[...]
            zz = z_ref[0, rows, cols[h]]
            y_ref[0, rows, cols[h]] = (on * _silu(zz)).astype(y_ref.dtype)
    for h in hs:
        state[h] = st[h]


def _deltanet(p3, ba3, hp, onw, *, heads, dh, col0):
    b, s, _ = p3.shape
    w = heads * dh
    assert s % DN_TILE == 0 and col0 % w == 0 and dh == LANES and DN_TILE == LANES
    cb = col0 // w
    t = DN_TILE
    return pl.pallas_call(
        functools.partial(_dn_kernel, heads=heads, dh=dh),
        grid=(b, s // t),
        in_specs=[
            pl.BlockSpec((1, t, w), lambda i, j: (i, j, cb)),
            pl.BlockSpec((1, t, w), lambda i, j: (i, j, cb + 1)),
            pl.BlockSpec((1, t, w), lambda i, j: (i, j, cb + 2)),
            pl.BlockSpec((1, t, w), lambda i, j: (i, j, cb + 3)),
            pl.BlockSpec((1, t, LANES), lambda i, j: (i, j, 0)),
            pl.BlockSpec((2 * heads, 2), lambda i, j: (0, 0)),
            pl.BlockSpec((1, dh), lambda i, j: (0, 0)),
        ],
        out_specs=pl.BlockSpec((1, t, w), lambda i, j: (i, j, 0)),
        out_shape=jax.ShapeDtypeStruct((b, s, w), BF16),
        scratch_shapes=[pltpu.VMEM((heads, dh, dh), F32)],
        compiler_params=pltpu.CompilerParams(
            dimension_semantics=("parallel", "arbitrary"), vmem_limit_bytes=VMEM_LIMIT),
        name="deltanet",
    )(p3, p3, p3, p3, ba3, hp, onw)


def _outproj_kernel(x_ref, mods_ref, yc_ref, yd_ref, wc_ref, wd_ref, o_ref, *, mod_idx):
    y = jnp.dot(yc_ref[...], wc_ref[...], preferred_element_type=F32)
    y = y + jnp.dot(yd_ref[...], wd_ref[...], preferred_element_type=F32)
    o_ref[...] = x_ref[...] + mods_ref[0, mod_idx:mod_idx + 1, :] * y


def _outproj(x2, mods, yc, yd, w_c, w_d, *, mod_idx, seq, tm=512):
    m, d = x2.shape
    tm = _tile(seq, tm)
    tiles_per_batch = seq // tm
    return pl.pallas_call(
        functools.partial(_outproj_kernel, mod_idx=mod_idx),
        grid=(m // tm,),
        in_specs=[
            pl.BlockSpec((tm, d), lambda i: (i, 0)),
            pl.BlockSpec((1, N_MOD, d), lambda i: (i // tiles_per_batch, 0, 0)),
            pl.BlockSpec((tm, yc.shape[1]), lambda i: (i, 0)),
            pl.BlockSpec((tm, yd.shape[1]), lambda i: (i, 0)),
            pl.BlockSpec(w_c.shape, lambda i: (0, 0)),
            pl.BlockSpec(w_d.shape, lambda i: (0, 0)),
        ],
        out_specs=pl.BlockSpec((tm, d), lambda i: (i, 0)),
        out_shape=jax.ShapeDtypeStruct((m, d), F32),
        compiler_params=pltpu.CompilerParams(
            dimension_semantics=("parallel",), vmem_limit_bytes=VMEM_LIMIT),
        name="out_proj",
    )(x2, mods, yc, yd, w_c, w_d)


def kernel(x, c, w_ada, b_ada, ffn1_norm, ffn1_wg, ffn1_wu, ffn1_wd, mix_norm, w_in, w_dw, b_dw,
           conv_ln_w, conv_ln_b, w_short, a_log, dt_bias, dn_norm_w, w_out, ffn2_norm, ffn2_wg,
           ffn2_wu, ffn2_wd, final_norm):
    b, s, d = x.shape
    depth = w_ada.shape[0]
    heads = a_log.shape[1]
    dh = dn_norm_w.shape[1]
    cw = w_dw.shape[2]
    dnw = heads * dh
    n_main = 2 * cw + 4 * dnw
    assert w_in.shape[2] == n_main + 2 * heads and 2 * heads <= LANES and cw == dnw and depth >= 1

    x2 = x.reshape(b * s, d)
    c_pad = jnp.pad(c, ((0, (-b) % HALO), (0, 0)))
    row = lambda v: v.reshape(1, -1)
    for l in range(depth):
        mods = _mods(c_pad, w_ada[l], row(b_ada[l]))[:b].reshape(b, N_MOD, d)
        last = l == depth - 1

        x2 = _ffn(x2, mods, row(ffn1_norm[l]), ffn1_wg[l].astype(BF16), ffn1_wu[l].astype(BF16),
                  ffn1_wd[l].astype(BF16), None, mod_base=0, seq=s)

        w_main = w_in[l][:, :n_main].astype(BF16)
        w_small = jnp.pad(w_in[l][:, n_main:], ((0, 0), (0, LANES - 2 * heads))).astype(BF16)
        p, ba = _inproj(x2, mods, row(mix_norm[l]), w_main, w_small, w_short[l], mod_base=3, seq=s,
                        dh=dh, conv0=2)
        p3 = p.reshape(b, s, n_main)
        ba3 = ba.reshape(b, s, LANES)

        y_conv = _conv_group(p3, w_dw[l], row(b_dw[l]), row(conv_ln_w[l]), row(conv_ln_b[l]))

        zh = jnp.zeros((heads,), F32)
        hp = jnp.stack([jnp.concatenate([zh, a_log[l]]), jnp.concatenate([zh, dt_bias[l]])], axis=1)
        y_dn = _deltanet(p3, ba3, hp, row(dn_norm_w[l]), heads=heads, dh=dh, col0=2 * cw)

        wo = w_out[l].astype(BF16)
        x2 = _outproj(x2, mods, y_conv.reshape(b * s, cw), y_dn.reshape(b * s, dnw),
                      wo[:cw], wo[cw:], mod_idx=5, seq=s)

        x2 = _ffn(x2, mods, row(ffn2_norm[l]), ffn2_wg[l].astype(BF16), ffn2_wu[l].astype(BF16),
                  ffn2_wd[l].astype(BF16), row(final_norm) if last else None, mod_base=6, seq=s)
    return x2.reshape(b, s, d)
```

```python
import functools

import jax
import jax.numpy as jnp
from jax import lax
from jax.experimental import pallas as pl
from jax.experimental.pallas import tpu as pltpu

F32 = jnp.float32
BF16 = jnp.bfloat16

EPS = 1e-6
N_MOD = 9
DN_CHUNK = 64
DN_TILE = 2 * DN_CHUNK
INV_BASE = 8
LANES = 128
HALO = 8
ROW_CHUNK = 32
ROW_UNROLL = 4
CONV_HALO = 32
VMEM_LIMIT = 60 * 1024 * 1024


def _tile(n, pref):
    t = min(n, pref)
    assert n % t == 0, (n, pref)
    return t


def _sigmoid(v):
    return jax.nn.sigmoid(v)


def _silu(v):
    return v * _sigmoid(v)


def _bdot(a, b):
    return jnp.dot(a.astype(BF16), b.astype(BF16), preferred_element_type=F32)


def _row_chunks(n_rows, fn):
    rc = min(ROW_CHUNK, n_rows)
    assert n_rows % rc == 0

    def body(i, carry):
        fn(pl.ds(pl.multiple_of(i * rc, rc), rc))
        return carry

    trips = n_rows // rc
    lax.fori_loop(0, trips, body, 0, unroll=min(ROW_UNROLL, trips))


def _norm_mod_store(x_ref, h_ref, nw, shift, scale):
    def chunk(rows):
        x = x_ref[rows, :]
        ms = jnp.mean(x * x, axis=-1, keepdims=True)
        y = x * lax.rsqrt(ms + EPS) * nw
        h_ref[rows, :] = (y * (1.0 + scale) + shift).astype(h_ref.dtype)

    _row_chunks(x_ref.shape[0], chunk)


def _mods_kernel(c_ref, w_ref, b_ref, o_ref):
    s = _silu(c_ref[...])
    o_ref[...] = _bdot(s, w_ref[...]) + b_ref[...]


def _mods(c_pad, w_ada, b_ada):
    rows, d = c_pad.shape
    n = w_ada.shape[1]
    tn = _tile(d, 1024)
    assert n % tn == 0
    return pl.pallas_call(
        _mods_kernel,
        grid=(n // tn,),
        in_specs=[
            pl.BlockSpec((rows, d), lambda j: (0, 0)),
            pl.BlockSpec((d, tn), lambda j: (0, j)),
            pl.BlockSpec((1, tn), lambda j: (0, j)),
        ],
        out_specs=pl.BlockSpec((rows, tn), lambda j: (0, j)),
        out_shape=jax.ShapeDtypeStruct((rows, n), F32),
        compiler_params=pltpu.CompilerParams(
            dimension_semantics=("parallel",), vmem_limit_bytes=VMEM_LIMIT),
        name="mods",
    )(c_pad, w_ada, b_ada)


def _ffn_kernel(x_ref, mods_ref, nw_ref, wg_ref, wu_ref, wd_ref, *rest, mod_base, final):
    if final:
        fnw_ref, o_ref, h_scr = rest
    else:
        o_ref, h_scr = rest
    f = pl.program_id(1)

    @pl.when(f == 0)
    def _():
        shift = mods_ref[0, mod_base:mod_base + 1, :]
        scale = mods_ref[0, mod_base + 1:mod_base + 2, :]
        _norm_mod_store(x_ref, h_scr, nw_ref[...], shift, scale)
        o_ref[...] = jnp.zeros_like(o_ref)

    h = h_scr[...]
    g = jnp.dot(h, wg_ref[...], preferred_element_type=F32)
    u = jnp.dot(h, wu_ref[...], preferred_element_type=F32)
    a = (_silu(g) * u).astype(BF16)
    o_ref[...] += jnp.dot(a, wd_ref[...], preferred_element_type=F32)

    @pl.when(f == pl.num_programs(1) - 1)
    def _():
        gate = mods_ref[0, mod_base + 2:mod_base + 3, :]

        def chunk(rows):
            y = x_ref[rows, :] + 0.5 * gate * o_ref[rows, :]
            if final:
                ms = jnp.mean(y * y, axis=-1, keepdims=True)
                y = y * lax.rsqrt(ms + EPS) * fnw_ref[...]
            o_ref[rows, :] = y

        _row_chunks(o_ref.shape[0], chunk)


def _ffn(x2, mods, nw, wg, wu, wd, fnw, *, mod_base, seq, tm=1024, tf=512):
    m, d = x2.shape
    dff = wg.shape[1]
    tm = _tile(seq, tm)
    tf = _tile(dff, tf)
    tiles_per_batch = seq // tm
    final = fnw is not None
    in_specs = [
        pl.BlockSpec((tm, d), lambda i, f: (i, 0)),
        pl.BlockSpec((1, N_MOD, d), lambda i, f: (i // tiles_per_batch, 0, 0)),
        pl.BlockSpec((1, d), lambda i, f: (0, 0)),
        pl.BlockSpec((d, tf), lambda i, f: (0, f)),
        pl.BlockSpec((d, tf), lambda i, f: (0, f)),
        pl.BlockSpec((tf, d), lambda i, f: (f, 0)),
    ]
    args = [x2, mods, nw, wg, wu, wd]
    if final:
        in_specs.append(pl.BlockSpec((1, d), lambda i, f: (0, 0)))
        args.append(fnw)
    return pl.pallas_call(
        functools.partial(_ffn_kernel, mod_base=mod_base, final=final),
        grid=(m // tm, dff // tf),
        in_specs=in_specs,
        out_specs=pl.BlockSpec((tm, d), lambda i, f: (i, 0)),
        out_shape=jax.ShapeDtypeStruct((m, d), F32),
        scratch_shapes=[pltpu.VMEM((tm, d), BF16)],
        compiler_params=pltpu.CompilerParams(
            dimension_semantics=("parallel", "arbitrary"), vmem_limit_bytes=VMEM_LIMIT),
        name="ffn_final" if final else "ffn",
    )(*args)


def _inproj_kernel(x_ref, mods_ref, nw_ref, w_ref, ws_ref, p_ref, ba_ref, h_scr, *, mod_base):
    n = pl.program_id(1)

    @pl.when(n == 0)
    def _():
        shift = mods_ref[0, mod_base:mod_base + 1, :]
        scale = mods_ref[0, mod_base + 1:mod_base + 2, :]
        _norm_mod_store(x_ref, h_scr, nw_ref[...], shift, scale)
        ba_ref[...] = jnp.dot(h_scr[...], ws_ref[...], preferred_element_type=F32)

    p_ref[...] = jnp.dot(h_scr[...], w_ref[...], preferred_element_type=F32)


def _inproj(x2, mods, nw, w_main, w_small, *, ncols, mod_base, seq, tm=1024, tn=1024):
    m, d = x2.shape
    tm = _tile(seq, tm)
    tn = min(tn, ncols // 6)
    assert ncols % tn == 0
    tiles_per_batch = seq // tm
    return pl.pallas_call(
        functools.partial(_inproj_kernel, mod_base=mod_base),
        grid=(m // tm, ncols // tn),
        in_specs=[
            pl.BlockSpec((tm, d), lambda i, n: (i, 0)),
            pl.BlockSpec((1, N_MOD, d), lambda i, n: (i // tiles_per_batch, 0, 0)),
            pl.BlockSpec((1, d), lambda i, n: (0, 0)),
            pl.BlockSpec((d, tn), lambda i, n: (0, n)),
            pl.BlockSpec((d, LANES), lambda i, n: (0, 0)),
        ],
        out_specs=[
            pl.BlockSpec((tm, tn), lambda i, n: (i, n)),
            pl.BlockSpec((tm, LANES), lambda i, n: (i, 0)),
        ],
        out_shape=[
            jax.ShapeDtypeStruct((m, ncols), F32),
            jax.ShapeDtypeStruct((m, LANES), F32),
        ],
        scratch_shapes=[pltpu.VMEM((tm, d), BF16)],
        compiler_params=pltpu.CompilerParams(
            dimension_semantics=("parallel", "arbitrary"), vmem_limit_bytes=VMEM_LIMIT),
        name="in_proj",
    )(x2, mods, nw, w_main, w_small)


def _conv_kernel(ca_ref, cg_ref, wdw_ref, bdw_ref, lnw_ref, lnb_ref, y_ref, buf, shf, *, taps, tc, rb):
    s = pl.program_id(1)

    @pl.when(s == 0)
    def _():
        buf[0:CONV_HALO, :] = jnp.zeros((CONV_HALO, buf.shape[1]), F32)

    buf[CONV_HALO:CONV_HALO + tc, :] = ca_ref[0] * _sigmoid(cg_ref[0])
    first = CONV_HALO - (taps - 1)
    nshift = shf.shape[1]
    for b in range(1, HALO):
        shf[b - 1] = buf[b:b + nshift, :]

    def window(off, r):
        a, b = divmod(off, HALO)
        if b == 0:
            return buf[HALO * a + r:HALO * a + r + rb, :]
        return shf[b - 1, HALO * a + r:HALO * a + r + rb, :]

    for r in range(0, tc, rb):
        acc = wdw_ref[0:1, :] * window(first, r)
        for j in range(1, taps):
            acc = acc + wdw_ref[j:j + 1, :] * window(first + j, r)
        hh = acc + bdw_ref[...]
        mu = jnp.mean(hh, axis=-1, keepdims=True)
        cen = hh - mu
        var = jnp.mean(cen * cen, axis=-1, keepdims=True)
        hn = cen * lax.rsqrt(var + EPS) * lnw_ref[...] + lnb_ref[...]
        y_ref[0, r:r + rb, :] = _silu(hn).astype(y_ref.dtype)
    buf[0:CONV_HALO, :] = buf[tc:tc + CONV_HALO, :]


def _conv_group(p3, w_dw, b_dw, ln_w, ln_b, *, tc=256, rb=32):
    b, s, _ = p3.shape
    taps, cw = w_dw.shape
    assert taps - 1 <= CONV_HALO
    tc = _tile(s, tc)
    rb = _tile(tc, rb)
    return pl.pallas_call(
        functools.partial(_conv_kernel, taps=taps, tc=tc, rb=rb),
        grid=(b, s // tc),
        in_specs=[
            pl.BlockSpec((1, tc, cw), lambda i, j: (i, j, 0)),
            pl.BlockSpec((1, tc, cw), lambda i, j: (i, j, 1)),
            pl.BlockSpec((taps, cw), lambda i, j: (0, 0)),
            pl.BlockSpec((1, cw), lambda i, j: (0, 0)),
            pl.BlockSpec((1, cw), lambda i, j: (0, 0)),
            pl.BlockSpec((1, cw), lambda i, j: (0, 0)),
        ],
        out_specs=pl.BlockSpec((1, tc, cw), lambda i, j: (i, j, 0)),
        out_shape=jax.ShapeDtypeStruct((b, s, cw), BF16),
        scratch_shapes=[pltpu.VMEM((tc + CONV_HALO, cw), F32),
                        pltpu.VMEM((HALO - 1, tc + CONV_HALO - HALO, cw), F32)],
        compiler_params=pltpu.CompilerParams(
            dimension_semantics=("parallel", "arbitrary"), vmem_limit_bytes=VMEM_LIMIT),
        name="conv_group",
    )(p3, p3, w_dw, b_dw, ln_w, ln_b)


def _dn_kernel(q_ref, k_ref, v_ref, z_ref, ba_ref, wsh_ref, hp_ref, onw_ref, y_ref, xbuf, state,
               *, heads, dh, taps):
    t, c = DN_TILE, DN_CHUNK
    ng = 2 * heads
    w = heads * dh
    s = pl.program_id(1)

    @pl.when(s == 0)
    def _():
        xbuf[0:HALO, :] = jnp.zeros((HALO, 3 * w), F32)
        state[...] = jnp.zeros_like(state)

    xbuf[HALO:HALO + t, 0:w] = q_ref[0]
    xbuf[HALO:HALO + t, w:2 * w] = k_ref[0]
    xbuf[HALO:HALO + t, 2 * w:3 * w] = v_ref[0]

    row = lax.broadcasted_iota(jnp.int32, (t, t), 0)
    col = lax.broadcasted_iota(jnp.int32, (t, t), 1)
    same = (row // c) == (col // c)
    tril = same & (row >= col)
    strict = same & (row > col)
    triu = same & (row <= col)
    eye = (row == col).astype(F32)
    blk = lambda n: (row // n) == (col // n)
    base_mask = blk(INV_BASE)
    merge_masks = []
    n = INV_BASE
    while n < c:
        merge_masks.append(blk(2 * n) & jnp.logical_not(blk(n)))
        n *= 2

    ba_t = ba_ref[0].T[0:ng, :]
    sig_t = _sigmoid(ba_t)
    xx = ba_t + hp_ref[:, 1:2]
    softplus = jnp.maximum(xx, 0.0) + jnp.log1p(jnp.exp(-jnp.abs(xx)))
    gg_t = -jnp.exp(hp_ref[:, 0:1]) * softplus
    gcum_t = jnp.dot(gg_t, triu.astype(F32), preferred_element_type=F32,
                     precision=lax.Precision.HIGHEST)
    rid = lax.broadcasted_iota(jnp.int32, (ng, t), 0)
    packed = jnp.where(rid < heads, sig_t, gcum_t)
    gate_cols = jnp.concatenate([packed, jnp.zeros((LANES - ng, t), F32)], axis=0).T

    hs = range(heads)
    cols = [slice(h * dh, (h + 1) * dh) for h in hs]

    def short_conv(base, h):
        lo = base + h * dh
        xs = xbuf[:, lo:lo + dh]
        acc = wsh_ref[0:1, lo:lo + dh] * xs
        for j in range(1, taps):
            acc = pltpu.roll(acc, 1, 0) + wsh_ref[j:j + 1, lo:lo + dh] * xs
        return _silu(acc[HALO:, :])

    def l2n(x, gain):
        return x * (lax.rsqrt(jnp.sum(x * x, axis=-1, keepdims=True) + EPS) * gain)

    q = [l2n(short_conv(0, h), dh ** -0.5) for h in hs]
    k = [l2n(short_conv(w, h), 1.0) for h in hs]
    v = [short_conv(2 * w, h) for h in hs]
    beta = [gate_cols[:, h:h + 1] for h in hs]
    g_col = [gate_cols[:, heads + h:heads + h + 1] for h in hs]
    e_g = [jnp.exp(g) for g in g_col]
    decay = [jnp.exp(jnp.where(tril, g_col[h] - gcum_t[heads + h:heads + h + 1, :], -jnp.inf))
             for h in hs]
    kb = [k[h] * beta[h] for h in hs]
    aq = [lax.dot_general(jnp.concatenate([kb[h], q[h]], axis=0).astype(BF16), k[h].astype(BF16),
                          (((1,), (1,)), ((), ())), preferred_element_type=F32) for h in hs]
    a_intra = [aq[h][t:] * decay[h] for h in hs]
    lmat = [aq[h][:t] * jnp.where(strict, decay[h], 0.0) for h in hs]

    xp = [-jnp.where(base_mask, lmat[h], 0.0) for h in hs]
    rr = [eye + xp[h] for h in hs]
    xp = [_bdot(xp[h], xp[h]) for h in hs]
    for _ in range(INV_BASE.bit_length() - 3):
        pr = [_bdot(jnp.concatenate([rr[h], xp[h]], axis=0), xp[h]) for h in hs]
        rr = [rr[h] + pr[h][:t] for h in hs]
        xp = [pr[h][t:] for h in hs]
    pr = [_bdot(rr[h], xp[h]) for h in hs]
    rr = [rr[h] + pr[h] for h in hs]
    for off_mask in merge_masks:
        pr = [_bdot(rr[h], jnp.where(off_mask, lmat[h], 0.0)) for h in hs]
        pr = [_bdot(pr[h], rr[h]) for h in hs]
        rr = [rr[h] - pr[h] for h in hs]

    sol = [_bdot(rr[h], jnp.concatenate([v[h] * beta[h], kb[h] * e_g[h]], axis=1)) for h in hs]
    qg = [q[h] * e_g[h] for h in hs]

    st = [state[h] for h in hs]
    for ci in range(t // c):
        rows = slice(ci * c, (ci + 1) * c)
        wq = [_bdot(jnp.concatenate([sol[h][rows, dh:], qg[h][rows]], axis=0), st[h]) for h in hs]
        v_new = [sol[h][rows, :dh] - wq[h][:c] for h in hs]
        g_last = [g_col[h][ci * c + c - 1:ci * c + c, :] for h in hs]
        kd = [k[h][rows] * jnp.exp(g_last[h] - g_col[h][rows]) for h in hs]
        upd = [lax.dot_general(kd[h].astype(BF16), v_new[h].astype(BF16),
                               (((0,), (0,)), ((), ())), preferred_element_type=F32) for h in hs]
        st = [st[h] * jnp.exp(g_last[h]) + upd[h] for h in hs]
        zeros = jnp.zeros((c, dh), F32)
        for h in hs:
            parts = [zeros] * (t // c)
            parts[ci] = v_new[h]
            v_pad = jnp.concatenate(parts, axis=0)
            o = wq[h][c:] + _bdot(a_intra[h][rows], v_pad)
            on = o * lax.rsqrt(jnp.mean(o * o, axis=-1, keepdims=True) + EPS) * onw_ref[...]
            zz = z_ref[0, rows, cols[h]]
            y_ref[0, rows, cols[h]] = (on * _silu(zz)).astype(y_ref.dtype)
    for h in hs:
        state[h] = st[h]
    xbuf[0:HALO, :] = xbuf[t:t + HALO, :]


def _deltanet(p3, ba3, w_short, hp, onw, *, heads, dh, col0):
    b, s, _ = p3.shape
    w = heads * dh
    taps = w_short.shape[0]
    assert s % DN_TILE == 0 and col0 % w == 0 and dh == LANES and DN_TILE == LANES
    assert taps - 1 <= HALO
    cb = col0 // w
    t = DN_TILE
    return pl.pallas_call(
        functools.partial(_dn_kernel, heads=heads, dh=dh, taps=taps),
        grid=(b, s // t),
        in_specs=[
            pl.BlockSpec((1, t, w), lambda i, j: (i, j, cb)),
            pl.BlockSpec((1, t, w), lambda i, j: (i, j, cb + 1)),
            pl.BlockSpec((1, t, w), lambda i, j: (i, j, cb + 2)),
            pl.BlockSpec((1, t, w), lambda i, j: (i, j, cb + 3)),
            pl.BlockSpec((1, t, LANES), lambda i, j: (i, j, 0)),
            pl.BlockSpec((taps, 3 * w), lambda i, j: (0, 0)),
            pl.BlockSpec((2 * heads, 2), lambda i, j: (0, 0)),
            pl.BlockSpec((1, dh), lambda i, j: (0, 0)),
        ],
        out_specs=pl.BlockSpec((1, t, w), lambda i, j: (i, j, 0)),
        out_shape=jax.ShapeDtypeStruct((b, s, w), BF16),
        scratch_shapes=[pltpu.VMEM((t + HALO, 3 * w), F32), pltpu.VMEM((heads, dh, dh), F32)],
        compiler_params=pltpu.CompilerParams(
            dimension_semantics=("parallel", "arbitrary"), vmem_limit_bytes=VMEM_LIMIT),
        name="deltanet",
    )(p3, p3, p3, p3, ba3, w_short, hp, onw)


def _outproj_kernel(x_ref, mods_ref, yc_ref, yd_ref, wc_ref, wd_ref, o_ref, *, mod_idx):
    y = jnp.dot(yc_ref[...], wc_ref[...], preferred_element_type=F32)
    y = y + jnp.dot(yd_ref[...], wd_ref[...], preferred_element_type=F32)
    o_ref[...] = x_ref[...] + mods_ref[0, mod_idx:mod_idx + 1, :] * y


def _outproj(x2, mods, yc, yd, w_c, w_d, *, mod_idx, seq, tm=512):
    m, d = x2.shape
    tm = _tile(seq, tm)
    tiles_per_batch = seq // tm
    return pl.pallas_call(
        functools.partial(_outproj_kernel, mod_idx=mod_idx),
        grid=(m // tm,),
        in_specs=[
            pl.BlockSpec((tm, d), lambda i: (i, 0)),
            pl.BlockSpec((1, N_MOD, d), lambda i: (i // tiles_per_batch, 0, 0)),
            pl.BlockSpec((tm, yc.shape[1]), lambda i: (i, 0)),
            pl.BlockSpec((tm, yd.shape[1]), lambda i: (i, 0)),
            pl.BlockSpec(w_c.shape, lambda i: (0, 0)),
            pl.BlockSpec(w_d.shape, lambda i: (0, 0)),
        ],
        out_specs=pl.BlockSpec((tm, d), lambda i: (i, 0)),
        out_shape=jax.ShapeDtypeStruct((m, d), F32),
        compiler_params=pltpu.CompilerParams(
            dimension_semantics=("parallel",), vmem_limit_bytes=VMEM_LIMIT),
        name="out_proj",
    )(x2, mods, yc, yd, w_c, w_d)


def kernel(x, c, w_ada, b_ada, ffn1_norm, ffn1_wg, ffn1_wu, ffn1_wd, mix_norm, w_in, w_dw, b_dw,
           conv_ln_w, conv_ln_b, w_short, a_log, dt_bias, dn_norm_w, w_out, ffn2_norm, ffn2_wg,
           ffn2_wu, ffn2_wd, final_norm):
    b, s, d = x.shape
    depth = w_ada.shape[0]
    heads = a_log.shape[1]
    dh = dn_norm_w.shape[1]
    cw = w_dw.shape[2]
    dnw = heads * dh
    n_main = 2 * cw + 4 * dnw
    assert w_in.shape[2] == n_main + 2 * heads and 2 * heads <= LANES and cw == dnw and depth >= 1

    x2 = x.reshape(b * s, d)
    c_pad = jnp.pad(c, ((0, (-b) % HALO), (0, 0)))
    row = lambda v: v.reshape(1, -1)
    for l in range(depth):
        mods = _mods(c_pad, w_ada[l], row(b_ada[l]))[:b].reshape(b, N_MOD, d)
        last = l == depth - 1

        x2 = _ffn(x2, mods, row(ffn1_norm[l]), ffn1_wg[l].astype(BF16), ffn1_wu[l].astype(BF16),
                  ffn1_wd[l].astype(BF16), None, mod_base=0, seq=s)

        w_main = w_in[l].astype(BF16)
        w_small = jnp.pad(w_in[l][:, n_main:], ((0, 0), (0, LANES - 2 * heads))).astype(BF16)
        p, ba = _inproj(x2, mods, row(mix_norm[l]), w_main, w_small, ncols=n_main,
                        mod_base=3, seq=s)
        p3 = p.reshape(b, s, n_main)
        ba3 = ba.reshape(b, s, LANES)

        y_conv = _conv_group(p3, w_dw[l], row(b_dw[l]), row(conv_ln_w[l]), row(conv_ln_b[l]))

        zh = jnp.zeros((heads,), F32)
        hp = jnp.stack([jnp.concatenate([zh, a_log[l]]), jnp.concatenate([zh, dt_bias[l]])], axis=1)
        y_dn = _deltanet(p3, ba3, w_short[l], hp, row(dn_norm_w[l]), heads=heads, dh=dh, col0=2 * cw)

        wo = w_out[l].astype(BF16)
        x2 = _outproj(x2, mods, y_conv.reshape(b * s, cw), y_dn.reshape(b * s, dnw),
                      wo[:cw], wo[cw:], mod_idx=5, seq=s)

        x2 = _ffn(x2, mods, row(ffn2_norm[l]), ffn2_wg[l].astype(BF16), ffn2_wu[l].astype(BF16),
                  ffn2_wd[l].astype(BF16), row(final_norm) if last else None, mod_base=6, seq=s)
    return x2.reshape(b, s, d)
```

```python
import functools

import jax
import jax.numpy as jnp
from jax import lax
from jax.experimental import pallas as pl
from jax.experimental.pallas import tpu as pltpu

F32 = jnp.float32
BF16 = jnp.bfloat16

EPS = 1e-6
N_MOD = 9
DN_CHUNK = 64
DN_TILE = 2 * DN_CHUNK
DN_PAIRS = 2
INV_BASE = 8
LANES = 128
HALO = 8
ROW_CHUNK = 32
ROW_UNROLL = 4
CONV_HALO = 32
VMEM_LIMIT = 60 * 1024 * 1024


def _tile(n, pref):
    t = min(n, pref)
    assert n % t == 0, (n, pref)
    return t


def _sigmoid(v):
    return jax.nn.sigmoid(v)


def _silu(v):
    return v * _sigmoid(v)


def _bdot(a, b):
    return jnp.dot(a.astype(BF16), b.astype(BF16), preferred_element_type=F32)


def _row_chunks(n_rows, fn):
    rc = min(ROW_CHUNK, n_rows)
    assert n_rows % rc == 0

    def body(i, carry):
        fn(pl.ds(pl.multiple_of(i * rc, rc), rc))
        return carry

    trips = n_rows // rc
    lax.fori_loop(0, trips, body, 0, unroll=min(ROW_UNROLL, trips))


def _norm_mod_store(x_ref, h_ref, nw, shift, scale):
    def chunk(rows):
        x = x_ref[rows, :]
        ms = jnp.mean(x * x, axis=-1, keepdims=True)
        y = x * lax.rsqrt(ms + EPS) * nw
        h_ref[rows, :] = (y * (1.0 + scale) + shift).astype(h_ref.dtype)

    _row_chunks(x_ref.shape[0], chunk)


def _mods_kernel(c_ref, w_ref, b_ref, o_ref):
    s = _silu(c_ref[...])
    o_ref[...] = _bdot(s, w_ref[...]) + b_ref[...]


def _mods(c_pad, w_ada, b_ada):
    rows, d = c_pad.shape
    n = w_ada.shape[1]
    tn = _tile(d, 1024)
    assert n % tn == 0
    return pl.pallas_call(
        _mods_kernel,
        grid=(n // tn,),
        in_specs=[
            pl.BlockSpec((rows, d), lambda j: (0, 0)),
            pl.BlockSpec((d, tn), lambda j: (0, j)),
            pl.BlockSpec((1, tn), lambda j: (0, j)),
        ],
        out_specs=pl.BlockSpec((rows, tn), lambda j: (0, j)),
        out_shape=jax.ShapeDtypeStruct((rows, n), F32),
        compiler_params=pltpu.CompilerParams(
            dimension_semantics=("parallel",), vmem_limit_bytes=VMEM_LIMIT),
        name="mods",
    )(c_pad, w_ada, b_ada)


def _ffn_kernel(x_ref, mods_ref, nw_ref, wg_ref, wu_ref, wd_ref, *rest, mod_base, final):
    if final:
        fnw_ref, o_ref, h_scr = rest
    else:
        o_ref, h_scr = rest
    f = pl.program_id(1)

    @pl.when(f == 0)
    def _():
        shift = mods_ref[0, mod_base:mod_base + 1, :]
        scale = mods_ref[0, mod_base + 1:mod_base + 2, :]
        _norm_mod_store(x_ref, h_scr, nw_ref[...], shift, scale)
        o_ref[...] = jnp.zeros_like(o_ref)

    h = h_scr[...]
    g = jnp.dot(h, wg_ref[...], preferred_element_type=F32)
    u = jnp.dot(h, wu_ref[...], preferred_element_type=F32)
    a = (_silu(g) * u).astype(BF16)
    o_ref[...] += jnp.dot(a, wd_ref[...], preferred_element_type=F32)

    @pl.when(f == pl.num_programs(1) - 1)
    def _():
        gate = mods_ref[0, mod_base + 2:mod_base + 3, :]

        def chunk(rows):
            y = x_ref[rows, :] + 0.5 * gate * o_ref[rows, :]
            if final:
                ms = jnp.mean(y * y, axis=-1, keepdims=True)
                y = y * lax.rsqrt(ms + EPS) * fnw_ref[...]
            o_ref[rows, :] = y

        _row_chunks(o_ref.shape[0], chunk)


def _ffn(x2, mods, nw, wg, wu, wd, fnw, *, mod_base, seq, tm=1024, tf=512):
    m, d = x2.shape
    dff = wg.shape[1]
    tm = _tile(seq, tm)
    tf = _tile(dff, tf)
    tiles_per_batch = seq // tm
    final = fnw is not None
    in_specs = [
        pl.BlockSpec((tm, d), lambda i, f: (i, 0)),
        pl.BlockSpec((1, N_MOD, d), lambda i, f: (i // tiles_per_batch, 0, 0)),
        pl.BlockSpec((1, d), lambda i, f: (0, 0)),
        pl.BlockSpec((d, tf), lambda i, f: (0, f)),
        pl.BlockSpec((d, tf), lambda i, f: (0, f)),
        pl.BlockSpec((tf, d), lambda i, f: (f, 0)),
    ]
    args = [x2, mods, nw, wg, wu, wd]
    if final:
        in_specs.append(pl.BlockSpec((1, d), lambda i, f: (0, 0)))
        args.append(fnw)
    return pl.pallas_call(
        functools.partial(_ffn_kernel, mod_base=mod_base, final=final),
        grid=(m // tm, dff // tf),
        in_specs=in_specs,
        out_specs=pl.BlockSpec((tm, d), lambda i, f: (i, 0)),
        out_shape=jax.ShapeDtypeStruct((m, d), F32),
        scratch_shapes=[pltpu.VMEM((tm, d), BF16)],
        compiler_params=pltpu.CompilerParams(
            dimension_semantics=("parallel", "arbitrary"), vmem_limit_bytes=VMEM_LIMIT),
        name="ffn_final" if final else "ffn",
    )(*args)


def _inproj_kernel(x_ref, mods_ref, nw_ref, w_ref, ws_ref, p_ref, ba_ref, h_scr, *, mod_base):
    n = pl.program_id(1)

    @pl.when(n == 0)
    def _():
        shift = mods_ref[0, mod_base:mod_base + 1, :]
        scale = mods_ref[0, mod_base + 1:mod_base + 2, :]
        _norm_mod_store(x_ref, h_scr, nw_ref[...], shift, scale)
        ba_ref[...] = jnp.dot(h_scr[...], ws_ref[...], preferred_element_type=F32)

    p_ref[...] = jnp.dot(h_scr[...], w_ref[...], preferred_element_type=F32)


def _inproj(x2, mods, nw, w_main, w_small, *, ncols, mod_base, seq, tm=1024, tn=1024):
    m, d = x2.shape
    tm = _tile(seq, tm)
    tn = min(tn, ncols // 6)
    assert ncols % tn == 0
    tiles_per_batch = seq // tm
    return pl.pallas_call(
        functools.partial(_inproj_kernel, mod_base=mod_base),
        grid=(m // tm, ncols // tn),
        in_specs=[
            pl.BlockSpec((tm, d), lambda i, n: (i, 0)),
            pl.BlockSpec((1, N_MOD, d), lambda i, n: (i // tiles_per_batch, 0, 0)),
            pl.BlockSpec((1, d), lambda i, n: (0, 0)),
            pl.BlockSpec((d, tn), lambda i, n: (0, n)),
            pl.BlockSpec((d, LANES), lambda i, n: (0, 0)),
        ],
        out_specs=[
            pl.BlockSpec((tm, tn), lambda i, n: (i, n)),
            pl.BlockSpec((tm, LANES), lambda i, n: (i, 0)),
        ],
        out_shape=[
            jax.ShapeDtypeStruct((m, ncols), F32),
            jax.ShapeDtypeStruct((m, LANES), F32),
        ],
        scratch_shapes=[pltpu.VMEM((tm, d), BF16)],
        compiler_params=pltpu.CompilerParams(
            dimension_semantics=("parallel", "arbitrary"), vmem_limit_bytes=VMEM_LIMIT),
        name="in_proj",
    )(x2, mods, nw, w_main, w_small)


def _conv_kernel(ca_ref, cg_ref, wdw_ref, bdw_ref, lnw_ref, lnb_ref, y_ref, buf, shf, *, taps, tc, rb):
    s = pl.program_id(1)

    @pl.when(s == 0)
    def _():
        buf[0:CONV_HALO, :] = jnp.zeros((CONV_HALO, buf.shape[1]), F32)

    buf[CONV_HALO:CONV_HALO + tc, :] = ca_ref[0] * _sigmoid(cg_ref[0])
    first = CONV_HALO - (taps - 1)
    nshift = shf.shape[1]
    for b in range(1, HALO):
        shf[b - 1] = buf[b:b + nshift, :]

    def window(off, r):
        a, b = divmod(off, HALO)
        if b == 0:
            return buf[HALO * a + r:HALO * a + r + rb, :]
        return shf[b - 1, HALO * a + r:HALO * a + r + rb, :]

    for r in range(0, tc, rb):
        acc = wdw_ref[0:1, :] * window(first, r)
        for j in range(1, taps):
            acc = acc + wdw_ref[j:j + 1, :] * window(first + j, r)
        hh = acc + bdw_ref[...]
        mu = jnp.mean(hh, axis=-1, keepdims=True)
        cen = hh - mu
        var = jnp.mean(cen * cen, axis=-1, keepdims=True)
        hn = cen * lax.rsqrt(var + EPS) * lnw_ref[...] + lnb_ref[...]
        y_ref[0, r:r + rb, :] = _silu(hn).astype(y_ref.dtype)
    buf[0:CONV_HALO, :] = buf[tc:tc + CONV_HALO, :]


def _conv_group(p3, w_dw, b_dw, ln_w, ln_b, *, tc=256, rb=32):
    b, s, _ = p3.shape
    taps, cw = w_dw.shape
    assert taps - 1 <= CONV_HALO
    tc = _tile(s, tc)
    rb = _tile(tc, rb)
    return pl.pallas_call(
        functools.partial(_conv_kernel, taps=taps, tc=tc, rb=rb),
        grid=(b, s // tc),
        in_specs=[
            pl.BlockSpec((1, tc, cw), lambda i, j: (i, j, 0)),
            pl.BlockSpec((1, tc, cw), lambda i, j: (i, j, 1)),
            pl.BlockSpec((taps, cw), lambda i, j: (0, 0)),
            pl.BlockSpec((1, cw), lambda i, j: (0, 0)),
            pl.BlockSpec((1, cw), lambda i, j: (0, 0)),
            pl.BlockSpec((1, cw), lambda i, j: (0, 0)),
        ],
        out_specs=pl.BlockSpec((1, tc, cw), lambda i, j: (i, j, 0)),
        out_shape=jax.ShapeDtypeStruct((b, s, cw), BF16),
        scratch_shapes=[pltpu.VMEM((tc + CONV_HALO, cw), F32),
                        pltpu.VMEM((HALO - 1, tc + CONV_HALO - HALO, cw), F32)],
        compiler_params=pltpu.CompilerParams(
            dimension_semantics=("parallel", "arbitrary"), vmem_limit_bytes=VMEM_LIMIT),
        name="conv_group",
    )(p3, p3, w_dw, b_dw, ln_w, ln_b)


def _dn_kernel(q0_ref, k0_ref, v0_ref, ba0_ref, qn_ref, kn_ref, vn_ref, ban_ref, z_ref, wsh_ref,
               hp_ref, onw_ref, y_ref, halo_s, qkv_s, gate_s, gt_s, state, *, heads, dh, taps,
               npairs):
    t, c = DN_TILE, DN_CHUNK
    ts = npairs * t
    ng = 2 * heads
    w = heads * dh
    j = pl.program_id(1)

    row = lax.broadcasted_iota(jnp.int32, (t, t), 0)
    col = lax.broadcasted_iota(jnp.int32, (t, t), 1)
    same = (row // c) == (col // c)
    tril = same & (row >= col)
    strict = same & (row > col)
    triu = same & (row <= col)
    eye = (row == col).astype(F32)
    blk = lambda n: (row // n) == (col // n)
    base_mask = blk(INV_BASE)
    merge_masks = []
    n = INV_BASE
    while n < c:
        merge_masks.append(blk(2 * n) & jnp.logical_not(blk(n)))
        n *= 2

    def make_tasks(q_src, k_src, v_src, ba_src):
        def gates_task(p):
            def run():
                ba_t = ba_src[0, p * t:(p + 1) * t, :].T[0:ng, :]
                sig_t = _sigmoid(ba_t)
                xx = ba_t + hp_ref[:, 1:2]
                softplus = jnp.maximum(xx, 0.0) + jnp.log1p(jnp.exp(-jnp.abs(xx)))
                gg_t = -jnp.exp(hp_ref[:, 0:1]) * softplus
                g_t = jnp.dot(gg_t, triu.astype(F32), preferred_element_type=F32,
                              precision=lax.Precision.HIGHEST)
                rid = lax.broadcasted_iota(jnp.int32, (ng, t), 0)
                packed = jnp.where(rid < heads, sig_t, g_t)
                gate_s[p * t:(p + 1) * t, :] = jnp.concatenate(
                    [packed, jnp.zeros((LANES - ng, t), F32)], axis=0).T
                gt_s[p] = g_t
            return run

        def conv_task(src, base, h, gain):
            lo = base + h * dh

            def run():
                xs = jnp.concatenate([halo_s[:, lo:lo + dh], src[0, :, h * dh:(h + 1) * dh]],
                                     axis=0)
                acc = wsh_ref[0:1, lo:lo + dh] * xs
                for tap in range(1, taps):
                    acc = pltpu.roll(acc, 1, 0) + wsh_ref[tap:tap + 1, lo:lo + dh] * xs
                y = _silu(acc[HALO:, :])
                if gain is not None:
                    y = y * (lax.rsqrt(jnp.sum(y * y, axis=-1, keepdims=True) + EPS) * gain)
                qkv_s[:, lo:lo + dh] = y
            return run

        def halo_task():
            halo_s[:, 0:w] = q_src[0, ts - HALO:ts, :]
            halo_s[:, w:2 * w] = k_src[0, ts - HALO:ts, :]
            halo_s[:, 2 * w:3 * w] = v_src[0, ts - HALO:ts, :]

        tasks = [gates_task(p) for p in range(npairs)]
        for h in range(heads):
            tasks += [conv_task(q_src, 0, h, dh ** -0.5), conv_task(k_src, w, h, 1.0),
                      conv_task(v_src, 2 * w, h, None)]
        tasks.append(halo_task)
        return tasks

    @pl.when(j == 0)
    def _():
        halo_s[...] = jnp.zeros_like(halo_s)
        state[...] = jnp.zeros_like(state)
        for task in make_tasks(q0_ref, k0_ref, v0_ref, ba0_ref):
            task()

    units = [(p, h) for p in range(npairs) for h in range(heads)]
    us = range(len(units))
    prow = lambda p: slice(p * t, (p + 1) * t)
    q = [qkv_s[prow(p), h * dh:(h + 1) * dh] for p, h in units]
    k = [qkv_s[prow(p), w + h * dh:w + (h + 1) * dh] for p, h in units]
    v = [qkv_s[prow(p), 2 * w + h * dh:2 * w + (h + 1) * dh] for p, h in units]
    gate_cols = [gate_s[prow(p), :] for p in range(npairs)]
    gcum_t = [gt_s[p] for p in range(npairs)]

    tasks = make_tasks(qn_ref, kn_ref, vn_ref, ban_ref)
    n_stages = 11 + 2 * npairs * (t // c)
    per_stage = -(-len(tasks) // n_stages)

    def prep():
        for _ in range(per_stage):
            if tasks:
                tasks.pop(0)()

    beta = [gate_cols[p][:, h:h + 1] for p, h in units]
    g_col = [gate_cols[p][:, heads + h:heads + h + 1] for p, h in units]
    g_row = [gcum_t[p][heads + h:heads + h + 1, :] for p, h in units]
    e_g = [jnp.exp(g) for g in g_col]
    decay = [jnp.exp(jnp.where(tril, g_col[u] - g_row[u], -jnp.inf)) for u in us]
    kb = [k[u] * beta[u] for u in us]
    aq = [lax.dot_general(jnp.concatenate([kb[u], q[u]], axis=0).astype(BF16), k[u].astype(BF16),
                          (((1,), (1,)), ((), ())), preferred_element_type=F32) for u in us]
    prep()
    a_intra = [aq[u][t:] * decay[u] for u in us]
    lmat = [aq[u][:t] * jnp.where(strict, decay[u], 0.0) for u in us]

    xp = [-jnp.where(base_mask, lmat[u], 0.0) for u in us]
    rr = [eye + xp[u] for u in us]
    xp = [_bdot(xp[u], xp[u]) for u in us]
    prep()
    for _ in range(INV_BASE.bit_length() - 3):
        pr = [_bdot(jnp.concatenate([rr[u], xp[u]], axis=0), xp[u]) for u in us]
        prep()
        rr = [rr[u] + pr[u][:t] for u in us]
        xp = [pr[u][t:] for u in us]
    pr = [_bdot(rr[u], xp[u]) for u in us]
    prep()
    rr = [rr[u] + pr[u] for u in us]
    for off_mask in merge_masks:
        pr = [_bdot(rr[u], jnp.where(off_mask, lmat[u], 0.0)) for u in us]
        prep()
        pr = [_bdot(pr[u], rr[u]) for u in us]
        prep()
        rr = [rr[u] - pr[u] for u in us]

    sol = [_bdot(rr[u], jnp.concatenate([v[u] * beta[u], kb[u] * e_g[u]], axis=1)) for u in us]
    prep()
    qg = [q[u] * e_g[u] for u in us]

    hs = range(heads)
    st = [state[h] for h in hs]
    zeros = jnp.zeros((c, dh), F32)
    for p in range(npairs):
        un = [p * heads + h for h in hs]
        for ci in range(t // c):
            rows = slice(ci * c, (ci + 1) * c)
            out_rows = slice(p * t + ci * c, p * t + (ci + 1) * c)
            wq = [_bdot(jnp.concatenate([sol[un[h]][rows, dh:], qg[un[h]][rows]], axis=0), st[h])
                  for h in hs]
            prep()
            v_new = [sol[un[h]][rows, :dh] - wq[h][:c] for h in hs]
            g_last = [g_col[un[h]][ci * c + c - 1:ci * c + c, :] for h in hs]
            kd = [k[un[h]][rows] * jnp.exp(g_last[h] - g_col[un[h]][rows]) for h in hs]
            upd = [lax.dot_general(kd[h].astype(BF16), v_new[h].astype(BF16),
                                   (((0,), (0,)), ((), ())), preferred_element_type=F32) for h in hs]
            prep()
            st = [st[h] * jnp.exp(g_last[h]) + upd[h] for h in hs]
            for h in hs:
                parts = [zeros] * (t // c)
                parts[ci] = v_new[h]
                v_pad = jnp.concatenate(parts, axis=0)
                o = wq[h][c:] + _bdot(a_intra[un[h]][rows], v_pad)
                on = o * lax.rsqrt(jnp.mean(o * o, axis=-1, keepdims=True) + EPS) * onw_ref[...]
                zz = z_ref[0, out_rows, h * dh:(h + 1) * dh]
                y_ref[0, out_rows, h * dh:(h + 1) * dh] = (on * _silu(zz)).astype(y_ref.dtype)
    while tasks:
        tasks.pop(0)()
    for h in hs:
        state[h] = st[h]


def _deltanet(p3, ba3, w_short, hp, onw, *, heads, dh, col0):
    b, s, _ = p3.shape
    w = heads * dh
    taps = w_short.shape[0]
    npairs = DN_PAIRS if s % (DN_PAIRS * DN_TILE) == 0 else 1
    ts = npairs * DN_TILE
    assert s % ts == 0 and col0 % w == 0 and dh == LANES and DN_TILE == LANES
    assert taps - 1 <= HALO
    cb = col0 // w
    nt = s // ts
    nxt = lambda j: jnp.minimum(j + 1, nt - 1)
    tile0 = lambda cc: pl.BlockSpec((1, ts, w), lambda i, j: (i, 0, cc))
    tilen = lambda cc: pl.BlockSpec((1, ts, w), lambda i, j: (i, nxt(j), cc))
    return pl.pallas_call(
        functools.partial(_dn_kernel, heads=heads, dh=dh, taps=taps, npairs=npairs),
        grid=(b, nt),
        in_specs=[
            tile0(cb), tile0(cb + 1), tile0(cb + 2),
            pl.BlockSpec((1, ts, LANES), lambda i, j: (i, 0, 0)),
            tilen(cb), tilen(cb + 1), tilen(cb + 2),
            pl.BlockSpec((1, ts, LANES), lambda i, j: (i, nxt(j), 0)),
            pl.BlockSpec((1, ts, w), lambda i, j: (i, j, cb + 3)),
            pl.BlockSpec((taps, 3 * w), lambda i, j: (0, 0)),
            pl.BlockSpec((2 * heads, 2), lambda i, j: (0, 0)),
            pl.BlockSpec((1, dh), lambda i, j: (0, 0)),
        ],
        out_specs=pl.BlockSpec((1, ts, w), lambda i, j: (i, j, 0)),
        out_shape=jax.ShapeDtypeStruct((b, s, w), BF16),
        scratch_shapes=[pltpu.VMEM((HALO, 3 * w), F32), pltpu.VMEM((ts, 3 * w), F32),
                        pltpu.VMEM((ts, LANES), F32), pltpu.VMEM((npairs, 2 * heads, DN_TILE), F32),
                        pltpu.VMEM((heads, dh, dh), F32)],
        compiler_params=pltpu.CompilerParams(
            dimension_semantics=("parallel", "arbitrary"), vmem_limit_bytes=VMEM_LIMIT),
        name="deltanet",
    )(p3, p3, p3, ba3, p3, p3, p3, ba3, p3, w_short, hp, onw)


def _outproj_kernel(x_ref, mods_ref, yc_ref, yd_ref, wc_ref, wd_ref, o_ref, *, mod_idx):
    y = jnp.dot(yc_ref[...], wc_ref[...], preferred_element_type=F32)
    y = y + jnp.dot(yd_ref[...], wd_ref[...], preferred_element_type=F32)
    o_ref[...] = x_ref[...] + mods_ref[0, mod_idx:mod_idx + 1, :] * y


def _outproj(x2, mods, yc, yd, w_c, w_d, *, mod_idx, seq, tm=512):
    m, d = x2.shape
    tm = _tile(seq, tm)
    tiles_per_batch = seq // tm
    return pl.pallas_call(
        functools.partial(_outproj_kernel, mod_idx=mod_idx),
        grid=(m // tm,),
        in_specs=[
            pl.BlockSpec((tm, d), lambda i: (i, 0)),
            pl.BlockSpec((1, N_MOD, d), lambda i: (i // tiles_per_batch, 0, 0)),
            pl.BlockSpec((tm, yc.shape[1]), lambda i: (i, 0)),
            pl.BlockSpec((tm, yd.shape[1]), lambda i: (i, 0)),
            pl.BlockSpec(w_c.shape, lambda i: (0, 0)),
            pl.BlockSpec(w_d.shape, lambda i: (0, 0)),
        ],
        out_specs=pl.BlockSpec((tm, d), lambda i: (i, 0)),
        out_shape=jax.ShapeDtypeStruct((m, d), F32),
        compiler_params=pltpu.CompilerParams(
            dimension_semantics=("parallel",), vmem_limit_bytes=VMEM_LIMIT),
        name="out_proj",
    )(x2, mods, yc, yd, w_c, w_d)


def kernel(x, c, w_ada, b_ada, ffn1_norm, ffn1_wg, ffn1_wu, ffn1_wd, mix_norm, w_in, w_dw, b_dw,
           conv_ln_w, conv_ln_b, w_short, a_log, dt_bias, dn_norm_w, w_out, ffn2_norm, ffn2_wg,
           ffn2_wu, ffn2_wd, final_norm):
    b, s, d = x.shape
    depth = w_ada.shape[0]
    heads = a_log.shape[1]
    dh = dn_norm_w.shape[1]
    cw = w_dw.shape[2]
    dnw = heads * dh
    n_main = 2 * cw + 4 * dnw
    assert w_in.shape[2] == n_main + 2 * heads and 2 * heads <= LANES and cw == dnw and depth >= 1

    x2 = x.reshape(b * s, d)
    c_pad = jnp.pad(c, ((0, (-b) % HALO), (0, 0)))
    row = lambda v: v.reshape(1, -1)
    for l in range(depth):
        mods = _mods(c_pad, w_ada[l], row(b_ada[l]))[:b].reshape(b, N_MOD, d)
        last = l == depth - 1

        x2 = _ffn(x2, mods, row(ffn1_norm[l]), ffn1_wg[l].astype(BF16), ffn1_wu[l].astype(BF16),
                  ffn1_wd[l].astype(BF16), None, mod_base=0, seq=s)

        w_main = w_in[l].astype(BF16)
        w_small = jnp.pad(w_in[l][:, n_main:], ((0, 0), (0, LANES - 2 * heads))).astype(BF16)
        p, ba = _inproj(x2, mods, row(mix_norm[l]), w_main, w_small, ncols=n_main,
                        mod_base=3, seq=s)
        p3 = p.reshape(b, s, n_main)
        ba3 = ba.reshape(b, s, LANES)

        y_conv = _conv_group(p3, w_dw[l], row(b_dw[l]), row(conv_ln_w[l]), row(conv_ln_b[l]))

        zh = jnp.zeros((heads,), F32)
        hp = jnp.stack([jnp.concatenate([zh, a_log[l]]), jnp.concatenate([zh, dt_bias[l]])], axis=1)
        y_dn = _deltanet(p3, ba3, w_short[l], hp, row(dn_norm_w[l]), heads=heads, dh=dh, col0=2 * cw)

        wo = w_out[l].astype(BF16)
        x2 = _outproj(x2, mods, y_conv.reshape(b * s, cw), y_dn.reshape(b * s, dnw),
                      wo[:cw], wo[cw:], mod_idx=5, seq=s)

        x2 = _ffn(x2, mods, row(ffn2_norm[l]), ffn2_wg[l].astype(BF16), ffn2_wu[l].astype(BF16),
                  ffn2_wd[l].astype(BF16), row(final_norm) if last else None, mod_base=6, seq=s)
    return x2.reshape(b, s, d)
```

```python
import functools

import jax
import jax.numpy as jnp
from jax import lax
from jax.experimental import pallas as pl
from jax.experimental.pallas import tpu as pltpu

F32 = jnp.float32
BF16 = jnp.bfloat16

EPS = 1e-6
N_MOD = 9
DN_CHUNK = 64
DN_TILE = 2 * DN_CHUNK
DN_PAIRS = 2
INV_BASE = 8
LANES = 128
HALO = 8
ROW_CHUNK = 32
ROW_UNROLL = 4
CONV_HALO = 32
VMEM_LIMIT = 60 * 1024 * 1024


def _tile(n, pref):
    t = min(n, pref)
    assert n % t == 0, (n, pref)
    return t


def _sigmoid(v):
    return jax.nn.sigmoid(v)


def _silu(v):
    return v * _sigmoid(v)


def _bdot(a, b):
    return jnp.dot(a.astype(BF16), b.astype(BF16), preferred_element_type=F32)


def _row_chunks(n_rows, fn):
    rc = min(ROW_CHUNK, n_rows)
    assert n_rows % rc == 0

    def body(i, carry):
        fn(pl.ds(pl.multiple_of(i * rc, rc), rc))
        return carry

    trips = n_rows // rc
    lax.fori_loop(0, trips, body, 0, unroll=min(ROW_UNROLL, trips))


def _norm_mod_store(x_ref, h_ref, nw, shift, scale):
    def chunk(rows):
        x = x_ref[rows, :]
        ms = jnp.mean(x * x, axis=-1, keepdims=True)
        y = x * lax.rsqrt(ms + EPS) * nw
        h_ref[rows, :] = (y * (1.0 + scale) + shift).astype(h_ref.dtype)

    _row_chunks(x_ref.shape[0], chunk)


def _mods_kernel(c_ref, w_ref, b_ref, o_ref):
    s = _silu(c_ref[...])
    o_ref[...] = _bdot(s, w_ref[...]) + b_ref[...]


def _mods(c_pad, w_ada, b_ada):
    rows, d = c_pad.shape
    n = w_ada.shape[1]
    tn = _tile(d, 1024)
    assert n % tn == 0
    return pl.pallas_call(
        _mods_kernel,
        grid=(n // tn,),
        in_specs=[
            pl.BlockSpec((rows, d), lambda j: (0, 0)),
            pl.BlockSpec((d, tn), lambda j: (0, j)),
            pl.BlockSpec((1, tn), lambda j: (0, j)),
        ],
        out_specs=pl.BlockSpec((rows, tn), lambda j: (0, j)),
        out_shape=jax.ShapeDtypeStruct((rows, n), F32),
        compiler_params=pltpu.CompilerParams(
            dimension_semantics=("parallel",), vmem_limit_bytes=VMEM_LIMIT),
        name="mods",
    )(c_pad, w_ada, b_ada)


def _ffn_kernel(x_ref, mods_ref, nw_ref, wg_ref, wu_ref, wd_ref, *rest, mod_base, final):
    if final:
        fnw_ref, o_ref, h_scr = rest
    else:
        o_ref, h_scr = rest
    f = pl.program_id(1)

    @pl.when(f == 0)
    def _():
        shift = mods_ref[0, mod_base:mod_base + 1, :]
        scale = mods_ref[0, mod_base + 1:mod_base + 2, :]
        _norm_mod_store(x_ref, h_scr, nw_ref[...], shift, scale)
        o_ref[...] = jnp.zeros_like(o_ref)

    h = h_scr[...]
    g = jnp.dot(h, wg_ref[...], preferred_element_type=F32)
    u = jnp.dot(h, wu_ref[...], preferred_element_type=F32)
    a = (_silu(g) * u).astype(BF16)
    o_ref[...] += jnp.dot(a, wd_ref[...], preferred_element_type=F32)

    @pl.when(f == pl.num_programs(1) - 1)
    def _():
        gate = mods_ref[0, mod_base + 2:mod_base + 3, :]

        def chunk(rows):
            y = x_ref[rows, :] + 0.5 * gate * o_ref[rows, :]
            if final:
                ms = jnp.mean(y * y, axis=-1, keepdims=True)
                y = y * lax.rsqrt(ms + EPS) * fnw_ref[...]
            o_ref[rows, :] = y

        _row_chunks(o_ref.shape[0], chunk)


def _ffn(x2, mods, nw, wg, wu, wd, fnw, *, mod_base, seq, tm=1024, tf=512):
    m, d = x2.shape
    dff = wg.shape[1]
    tm = _tile(seq, tm)
    tf = _tile(dff, tf)
    tiles_per_batch = seq // tm
    final = fnw is not None
    in_specs = [
        pl.BlockSpec((tm, d), lambda i, f: (i, 0)),
        pl.BlockSpec((1, N_MOD, d), lambda i, f: (i // tiles_per_batch, 0, 0)),
        pl.BlockSpec((1, d), lambda i, f: (0, 0)),
        pl.BlockSpec((d, tf), lambda i, f: (0, f)),
        pl.BlockSpec((d, tf), lambda i, f: (0, f)),
        pl.BlockSpec((tf, d), lambda i, f: (f, 0)),
    ]
    args = [x2, mods, nw, wg, wu, wd]
    if final:
        in_specs.append(pl.BlockSpec((1, d), lambda i, f: (0, 0)))
        args.append(fnw)
    return pl.pallas_call(
        functools.partial(_ffn_kernel, mod_base=mod_base, final=final),
        grid=(m // tm, dff // tf),
        in_specs=in_specs,
        out_specs=pl.BlockSpec((tm, d), lambda i, f: (i, 0)),
        out_shape=jax.ShapeDtypeStruct((m, d), F32),
        scratch_shapes=[pltpu.VMEM((tm, d), BF16)],
        compiler_params=pltpu.CompilerParams(
            dimension_semantics=("parallel", "arbitrary"), vmem_limit_bytes=VMEM_LIMIT),
        name="ffn_final" if final else "ffn",
    )(*args)


def _inproj_kernel(x_ref, mods_ref, nw_ref, w_ref, ws_ref, wdw_ref, bdw_ref, lnw_ref, lnb_ref,
                   p_ref, ba_ref, y_ref, h_scr, ca_s, buf, shf, *, mod_base, tiles_per_batch, taps,
                   rb):
    i = pl.program_id(0)
    n = pl.program_id(1)
    tm = h_scr.shape[0]
    tq = tm // 4
    nshift = shf.shape[1]
    first = CONV_HALO - (taps - 1)

    @pl.when(n == 0)
    def _():
        shift = mods_ref[0, mod_base:mod_base + 1, :]
        scale = mods_ref[0, mod_base + 1:mod_base + 2, :]
        _norm_mod_store(x_ref, h_scr, nw_ref[...], shift, scale)
        ba_ref[...] = jnp.dot(h_scr[...], ws_ref[...], preferred_element_type=F32)
        ca_s[...] = jnp.dot(h_scr[...], w_ref[...], preferred_element_type=F32)

    @pl.when(n == 1)
    def _():
        @pl.when(i % tiles_per_batch == 0)
        def _():
            buf[0:CONV_HALO, :] = jnp.zeros((CONV_HALO, buf.shape[1]), F32)

        cg = jnp.dot(h_scr[...], w_ref[...], preferred_element_type=F32)
        buf[CONV_HALO:CONV_HALO + tm, :] = ca_s[...] * _sigmoid(cg)

    def conv_rows(r0, r):
        def window(off):
            a, b = divmod(off, HALO)
            if b == 0:
                return buf[r0 + HALO * a + r:r0 + HALO * a + r + rb, :]
            return shf[b - 1, HALO * a + r:HALO * a + r + rb, :]

        acc = wdw_ref[0:1, :] * window(first)
        for tap in range(1, taps):
            acc = acc + wdw_ref[tap:tap + 1, :] * window(first + tap)
        hh = acc + bdw_ref[...]
        mu = jnp.mean(hh, axis=-1, keepdims=True)
        cen = hh - mu
        var = jnp.mean(cen * cen, axis=-1, keepdims=True)
        hn = cen * lax.rsqrt(var + EPS) * lnw_ref[...] + lnb_ref[...]
        y_ref[r0 + r:r0 + r + rb, :] = _silu(hn).astype(y_ref.dtype)

    for qn in range(4):
        @pl.when(n == 2 + qn)
        def _(qn=qn):
            r0 = qn * tq
            for b in range(1, HALO):
                shf[b - 1] = buf[r0 + b:r0 + b + nshift, :]
            p_ref[...] = jnp.dot(h_scr[...], w_ref[...], preferred_element_type=F32)
            for r in range(0, tq, rb):
                conv_rows(r0, r)
            if qn == 3:
                buf[0:CONV_HALO, :] = buf[tm:tm + CONV_HALO, :]


def _inproj(x2, mods, nw, w_main, w_small, w_dw, b_dw, ln_w, ln_b, *, mod_base, seq, tm=512, rb=32):
    m, d = x2.shape
    taps, cw = w_dw.shape
    assert taps - 1 <= CONV_HALO
    tm = _tile(seq, tm)
    tq = tm // 4
    rb = _tile(tq, rb)
    assert tq % HALO == 0
    tiles_per_batch = seq // tm
    vec = lambda: pl.BlockSpec((1, cw), lambda i, n: (0, 0))
    return pl.pallas_call(
        functools.partial(_inproj_kernel, mod_base=mod_base, tiles_per_batch=tiles_per_batch,
                          taps=taps, rb=rb),
        grid=(m // tm, 6),
        in_specs=[
            pl.BlockSpec((tm, d), lambda i, n: (i, 0)),
            pl.BlockSpec((1, N_MOD, d), lambda i, n: (i // tiles_per_batch, 0, 0)),
            pl.BlockSpec((1, d), lambda i, n: (0, 0)),
            pl.BlockSpec((d, cw), lambda i, n: (0, n)),
            pl.BlockSpec((d, LANES), lambda i, n: (0, 0)),
            pl.BlockSpec((taps, cw), lambda i, n: (0, 0)),
            vec(), vec(), vec(),
        ],
        out_specs=[
            pl.BlockSpec((tm, cw), lambda i, n: (i, jnp.maximum(n - 2, 0))),
            pl.BlockSpec((tm, LANES), lambda i, n: (i, 0)),
            pl.BlockSpec((tm, cw), lambda i, n: (i, 0)),
        ],
        out_shape=[
            jax.ShapeDtypeStruct((m, 4 * cw), F32),
            jax.ShapeDtypeStruct((m, LANES), F32),
            jax.ShapeDtypeStruct((m, cw), BF16),
        ],
        scratch_shapes=[pltpu.VMEM((tm, d), BF16), pltpu.VMEM((tm, cw), F32),
                        pltpu.VMEM((tm + CONV_HALO, cw), F32),
                        pltpu.VMEM((HALO - 1, tq + CONV_HALO - HALO, cw), F32)],
        compiler_params=pltpu.CompilerParams(
            dimension_semantics=("arbitrary", "arbitrary"), vmem_limit_bytes=VMEM_LIMIT),
        name="in_proj",
    )(x2, mods, nw, w_main, w_small, w_dw, b_dw, ln_w, ln_b)


def _dn_kernel(q0_ref, k0_ref, v0_ref, ba0_ref, qn_ref, kn_ref, vn_ref, ban_ref, z_ref, wsh_ref,
               hp_ref, onw_ref, y_ref, halo_s, qkv_s, gate_s, gt_s, state, *, heads, dh, taps,
               npairs):
    t, c = DN_TILE, DN_CHUNK
    ts = npairs * t
    ng = 2 * heads
    w = heads * dh
    j = pl.program_id(1)

    row = lax.broadcasted_iota(jnp.int32, (t, t), 0)
    col = lax.broadcasted_iota(jnp.int32, (t, t), 1)
    same = (row // c) == (col // c)
    tril = same & (row >= col)
    strict = same & (row > col)
    triu = same & (row <= col)
    eye = (row == col).astype(F32)
    blk = lambda n: (row // n) == (col // n)
    base_mask = blk(INV_BASE)
    merge_masks = []
    n = INV_BASE
    while n < c:
        merge_masks.append(blk(2 * n) & jnp.logical_not(blk(n)))
        n *= 2

    def make_tasks(q_src, k_src, v_src, ba_src):
        def gates_task(p):
            def run():
                ba_t = ba_src[0, p * t:(p + 1) * t, :].T[0:ng, :]
                sig_t = _sigmoid(ba_t)
                xx = ba_t + hp_ref[:, 1:2]
                softplus = jnp.maximum(xx, 0.0) + jnp.log1p(jnp.exp(-jnp.abs(xx)))
                gg_t = -jnp.exp(hp_ref[:, 0:1]) * softplus
                g_t = jnp.dot(gg_t, triu.astype(F32), preferred_element_type=F32,
                              precision=lax.Precision.HIGHEST)
                rid = lax.broadcasted_iota(jnp.int32, (ng, t), 0)
                packed = jnp.where(rid < heads, sig_t, g_t)
                gate_s[p * t:(p + 1) * t, :] = jnp.concatenate(
                    [packed, jnp.zeros((LANES - ng, t), F32)], axis=0).T
                gt_s[p] = g_t
            return run

        def conv_task(src, base, h, gain):
            lo = base + h * dh

            def run():
                xs = jnp.concatenate([halo_s[:, lo:lo + dh], src[0, :, h * dh:(h + 1) * dh]],
                                     axis=0)
                acc = wsh_ref[0:1, lo:lo + dh] * xs
                for tap in range(1, taps):
                    acc = pltpu.roll(acc, 1, 0) + wsh_ref[tap:tap + 1, lo:lo + dh] * xs
                y = _silu(acc[HALO:, :])
                if gain is not None:
                    y = y * (lax.rsqrt(jnp.sum(y * y, axis=-1, keepdims=True) + EPS) * gain)
                qkv_s[:, lo:lo + dh] = y
            return run

        def halo_task():
            halo_s[:, 0:w] = q_src[0, ts - HALO:ts, :]
            halo_s[:, w:2 * w] = k_src[0, ts - HALO:ts, :]
            halo_s[:, 2 * w:3 * w] = v_src[0, ts - HALO:ts, :]

        tasks = [gates_task(p) for p in range(npairs)]
        for h in range(heads):
            tasks += [conv_task(q_src, 0, h, dh ** -0.5), conv_task(k_src, w, h, 1.0),
                      conv_task(v_src, 2 * w, h, None)]
        tasks.append(halo_task)
        return tasks

    @pl.when(j == 0)
    def _():
        halo_s[...] = jnp.zeros_like(halo_s)
        state[...] = jnp.zeros_like(state)
        for task in make_tasks(q0_ref, k0_ref, v0_ref, ba0_ref):
            task()

    units = [(p, h) for p in range(npairs) for h in range(heads)]
    us = range(len(units))
    prow = lambda p: slice(p * t, (p + 1) * t)
    q = [qkv_s[prow(p), h * dh:(h + 1) * dh] for p, h in units]
    k = [qkv_s[prow(p), w + h * dh:w + (h + 1) * dh] for p, h in units]
    v = [qkv_s[prow(p), 2 * w + h * dh:2 * w + (h + 1) * dh] for p, h in units]
    gate_cols = [gate_s[prow(p), :] for p in range(npairs)]
    gcum_t = [gt_s[p] for p in range(npairs)]

    tasks = make_tasks(qn_ref, kn_ref, vn_ref, ban_ref)
    n_stages = 11 + 2 * npairs * (t // c)
    per_stage = -(-len(tasks) // n_stages)

    def prep():
        for _ in range(per_stage):
            if tasks:
                tasks.pop(0)()

    beta = [gate_cols[p][:, h:h + 1] for p, h in units]
    g_col = [gate_cols[p][:, heads + h:heads + h + 1] for p, h in units]
    g_row = [gcum_t[p][heads + h:heads + h + 1, :] for p, h in units]
    e_g = [jnp.exp(g) for g in g_col]
    decay = [jnp.exp(jnp.where(tril, g_col[u] - g_row[u], -jnp.inf)) for u in us]
    kb = [k[u] * beta[u] for u in us]
    aq = [lax.dot_general(jnp.concatenate([kb[u], q[u]], axis=0).astype(BF16), k[u].astype(BF16),
                          (((1,), (1,)), ((), ())), preferred_element_type=F32) for u in us]
    prep()
    a_intra = [aq[u][t:] * decay[u] for u in us]
    lmat = [aq[u][:t] * jnp.where(strict, decay[u], 0.0) for u in us]

    xp = [-jnp.where(base_mask, lmat[u], 0.0) for u in us]
    rr = [eye + xp[u] for u in us]
    xp = [_bdot(xp[u], xp[u]) for u in us]
    prep()
    for _ in range(INV_BASE.bit_length() - 3):
        pr = [_bdot(jnp.concatenate([rr[u], xp[u]], axis=0), xp[u]) for u in us]
        prep()
        rr = [rr[u] + pr[u][:t] for u in us]
        xp = [pr[u][t:] for u in us]
    pr = [_bdot(rr[u], xp[u]) for u in us]
    prep()
    rr = [rr[u] + pr[u] for u in us]
    for off_mask in merge_masks:
        pr = [_bdot(rr[u], jnp.where(off_mask, lmat[u], 0.0)) for u in us]
        prep()
        pr = [_bdot(pr[u], rr[u]) for u in us]
        prep()
        rr = [rr[u] - pr[u] for u in us]

    sol = [_bdot(rr[u], jnp.concatenate([v[u] * beta[u], kb[u] * e_g[u]], axis=1)) for u in us]
    prep()
    qg = [q[u] * e_g[u] for u in us]

    hs = range(heads)
    st = [state[h] for h in hs]
    zeros = jnp.zeros((c, dh), F32)
    for p in range(npairs):
        un = [p * heads + h for h in hs]
        for ci in range(t // c):
            rows = slice(ci * c, (ci + 1) * c)
            out_rows = slice(p * t + ci * c, p * t + (ci + 1) * c)
            wq = [_bdot(jnp.concatenate([sol[un[h]][rows, dh:], qg[un[h]][rows]], axis=0), st[h])
                  for h in hs]
            prep()
            v_new = [sol[un[h]][rows, :dh] - wq[h][:c] for h in hs]
            g_last = [g_col[un[h]][ci * c + c - 1:ci * c + c, :] for h in hs]
            kd = [k[un[h]][rows] * jnp.exp(g_last[h] - g_col[un[h]][rows]) for h in hs]
            upd = [lax.dot_general(kd[h].astype(BF16), v_new[h].astype(BF16),
                                   (((0,), (0,)), ((), ())), preferred_element_type=F32) for h in hs]
            prep()
            st = [st[h] * jnp.exp(g_last[h]) + upd[h] for h in hs]
            for h in hs:
                parts = [zeros] * (t // c)
                parts[ci] = v_new[h]
                v_pad = jnp.concatenate(parts, axis=0)
                o = wq[h][c:] + _bdot(a_intra[un[h]][rows], v_pad)
                on = o * lax.rsqrt(jnp.mean(o * o, axis=-1, keepdims=True) + EPS) * onw_ref[...]
                zz = z_ref[0, out_rows, h * dh:(h + 1) * dh]
                y_ref[0, out_rows, h * dh:(h + 1) * dh] = (on * _silu(zz)).astype(y_ref.dtype)
    while tasks:
        tasks.pop(0)()
    for h in hs:
        state[h] = st[h]


def _deltanet(p3, ba3, w_short, hp, onw, *, heads, dh, col0):
    b, s, _ = p3.shape
    w = heads * dh
    taps = w_short.shape[0]
    npairs = DN_PAIRS if s % (DN_PAIRS * DN_TILE) == 0 else 1
    ts = npairs * DN_TILE
    assert s % ts == 0 and col0 % w == 0 and dh == LANES and DN_TILE == LANES
    assert taps - 1 <= HALO
    cb = col0 // w
    nt = s // ts
    nxt = lambda j: jnp.minimum(j + 1, nt - 1)
    tile0 = lambda cc: pl.BlockSpec((1, ts, w), lambda i, j: (i, 0, cc))
    tilen = lambda cc: pl.BlockSpec((1, ts, w), lambda i, j: (i, nxt(j), cc))
    return pl.pallas_call(
        functools.partial(_dn_kernel, heads=heads, dh=dh, taps=taps, npairs=npairs),
        grid=(b, nt),
        in_specs=[
            tile0(cb), tile0(cb + 1), tile0(cb + 2),
            pl.BlockSpec((1, ts, LANES), lambda i, j: (i, 0, 0)),
            tilen(cb), tilen(cb + 1), tilen(cb + 2),
            pl.BlockSpec((1, ts, LANES), lambda i, j: (i, nxt(j), 0)),
            pl.BlockSpec((1, ts, w), lambda i, j: (i, j, cb + 3)),
            pl.BlockSpec((taps, 3 * w), lambda i, j: (0, 0)),
            pl.BlockSpec((2 * heads, 2), lambda i, j: (0, 0)),
            pl.BlockSpec((1, dh), lambda i, j: (0, 0)),
        ],
        out_specs=pl.BlockSpec((1, ts, w), lambda i, j: (i, j, 0)),
        out_shape=jax.ShapeDtypeStruct((b, s, w), BF16),
        scratch_shapes=[pltpu.VMEM((HALO, 3 * w), F32), pltpu.VMEM((ts, 3 * w), F32),
                        pltpu.VMEM((ts, LANES), F32), pltpu.VMEM((npairs, 2 * heads, DN_TILE), F32),
                        pltpu.VMEM((heads, dh, dh), F32)],
        compiler_params=pltpu.CompilerParams(
            dimension_semantics=("parallel", "arbitrary"), vmem_limit_bytes=VMEM_LIMIT),
        name="deltanet",
    )(p3, p3, p3, ba3, p3, p3, p3, ba3, p3, w_short, hp, onw)


def _outproj_kernel(x_ref, mods_ref, yc_ref, yd_ref, wc_ref, wd_ref, o_ref, *, mod_idx):
    y = jnp.dot(yc_ref[...], wc_ref[...], preferred_element_type=F32)
    y = y + jnp.dot(yd_ref[...], wd_ref[...], preferred_element_type=F32)
    o_ref[...] = x_ref[...] + mods_ref[0, mod_idx:mod_idx + 1, :] * y


def _outproj(x2, mods, yc, yd, w_c, w_d, *, mod_idx, seq, tm=512):
    m, d = x2.shape
    tm = _tile(seq, tm)
    tiles_per_batch = seq // tm
    return pl.pallas_call(
        functools.partial(_outproj_kernel, mod_idx=mod_idx),
        grid=(m // tm,),
        in_specs=[
            pl.BlockSpec((tm, d), lambda i: (i, 0)),
            pl.BlockSpec((1, N_MOD, d), lambda i: (i // tiles_per_batch, 0, 0)),
            pl.BlockSpec((tm, yc.shape[1]), lambda i: (i, 0)),
            pl.BlockSpec((tm, yd.shape[1]), lambda i: (i, 0)),
            pl.BlockSpec(w_c.shape, lambda i: (0, 0)),
            pl.BlockSpec(w_d.shape, lambda i: (0, 0)),
        ],
        out_specs=pl.BlockSpec((tm, d), lambda i: (i, 0)),
        out_shape=jax.ShapeDtypeStruct((m, d), F32),
        compiler_params=pltpu.CompilerParams(
            dimension_semantics=("parallel",), vmem_limit_bytes=VMEM_LIMIT),
        name="out_proj",
    )(x2, mods, yc, yd, w_c, w_d)


def kernel(x, c, w_ada, b_ada, ffn1_norm, ffn1_wg, ffn1_wu, ffn1_wd, mix_norm, w_in, w_dw, b_dw,
           conv_ln_w, conv_ln_b, w_short, a_log, dt_bias, dn_norm_w, w_out, ffn2_norm, ffn2_wg,
           ffn2_wu, ffn2_wd, final_norm):
    b, s, d = x.shape
    depth = w_ada.shape[0]
    heads = a_log.shape[1]
    dh = dn_norm_w.shape[1]
    cw = w_dw.shape[2]
    dnw = heads * dh
    n_main = 2 * cw + 4 * dnw
    assert w_in.shape[2] == n_main + 2 * heads and 2 * heads <= LANES and cw == dnw and depth >= 1

    x2 = x.reshape(b * s, d)
    c_pad = jnp.pad(c, ((0, (-b) % HALO), (0, 0)))
    row = lambda v: v.reshape(1, -1)
    for l in range(depth):
        mods = _mods(c_pad, w_ada[l], row(b_ada[l]))[:b].reshape(b, N_MOD, d)
        last = l == depth - 1

        x2 = _ffn(x2, mods, row(ffn1_norm[l]), ffn1_wg[l].astype(BF16), ffn1_wu[l].astype(BF16),
                  ffn1_wd[l].astype(BF16), None, mod_base=0, seq=s)

        w_main = w_in[l].astype(BF16)
        w_small = jnp.pad(w_in[l][:, n_main:], ((0, 0), (0, LANES - 2 * heads))).astype(BF16)
        p, ba, y_conv = _inproj(x2, mods, row(mix_norm[l]), w_main, w_small, w_dw[l], row(b_dw[l]),
                                row(conv_ln_w[l]), row(conv_ln_b[l]), mod_base=3, seq=s)
        p3 = p.reshape(b, s, 4 * dnw)
        ba3 = ba.reshape(b, s, LANES)

        zh = jnp.zeros((heads,), F32)
        hp = jnp.stack([jnp.concatenate([zh, a_log[l]]), jnp.concatenate([zh, dt_bias[l]])], axis=1)
        y_dn = _deltanet(p3, ba3, w_short[l], hp, row(dn_norm_w[l]), heads=heads, dh=dh, col0=0)

        wo = w_out[l].astype(BF16)
        x2 = _outproj(x2, mods, y_conv, y_dn.reshape(b * s, dnw),
                      wo[:cw], wo[cw:], mod_idx=5, seq=s)

        x2 = _ffn(x2, mods, row(ffn2_norm[l]), ffn2_wg[l].astype(BF16), ffn2_wu[l].astype(BF16),
                  ffn2_wd[l].astype(BF16), row(final_norm) if last else None, mod_base=6, seq=s)
    return x2.reshape(b, s, d)
```

```python
import functools

import jax
import jax.numpy as jnp
from jax import lax
from jax.experimental import pallas as pl
from jax.experimental.pallas import tpu as pltpu

F32 = jnp.float32
BF16 = jnp.bfloat16

EPS = 1e-6
N_MOD = 9
DN_CHUNK = 64
DN_TILE = 2 * DN_CHUNK
DN_PAIRS = 2
INV_BASE = 8
LANES = 128
HALO = 8
ROW_CHUNK = 32
ROW_UNROLL = 4
CONV_HALO = 32
VMEM_LIMIT = 60 * 1024 * 1024


def _tile(n, pref):
    t = min(n, pref)
    assert n % t == 0, (n, pref)
    return t


def _sigmoid(v):
    return jax.nn.sigmoid(v)


def _silu(v):
    return v * _sigmoid(v)


def _bdot(a, b):
    return jnp.dot(a.astype(BF16), b.astype(BF16), preferred_element_type=F32)


def _row_chunks(n_rows, fn):
    rc = min(ROW_CHUNK, n_rows)
    assert n_rows % rc == 0

    def body(i, carry):
        fn(pl.ds(pl.multiple_of(i * rc, rc), rc))
        return carry

    trips = n_rows // rc
    lax.fori_loop(0, trips, body, 0, unroll=min(ROW_UNROLL, trips))


def _norm_mod_store(x_ref, h_ref, nw, shift, scale):
    gain = nw * (1.0 + scale)

    def chunk(rows):
        x = x_ref[rows, :]
        ms = jnp.mean(x * x, axis=-1, keepdims=True)
        h_ref[rows, :] = (x * lax.rsqrt(ms + EPS) * gain + shift).astype(h_ref.dtype)

    _row_chunks(x_ref.shape[0], chunk)


def _mods_kernel(c_ref, w_ref, b_ref, o_ref):
    s = _silu(c_ref[...])
    o_ref[...] = _bdot(s, w_ref[...]) + b_ref[...]


def _mods(c_pad, w_ada, b_ada):
    rows, d = c_pad.shape
    n = w_ada.shape[1]
    tn = _tile(d, 1024)
    assert n % tn == 0
    return pl.pallas_call(
        _mods_kernel,
        grid=(n // tn,),
        in_specs=[
            pl.BlockSpec((rows, d), lambda j: (0, 0)),
            pl.BlockSpec((d, tn), lambda j: (0, j)),
            pl.BlockSpec((1, tn), lambda j: (0, j)),
        ],
        out_specs=pl.BlockSpec((rows, tn), lambda j: (0, j)),
        out_shape=jax.ShapeDtypeStruct((rows, n), F32),
        compiler_params=pltpu.CompilerParams(
            dimension_semantics=("parallel",), vmem_limit_bytes=VMEM_LIMIT),
        name="mods",
    )(c_pad, w_ada, b_ada)


def _ffn_kernel(x_ref, mods_ref, nw_ref, wg_ref, wu_ref, wd_ref, *rest, mod_base, final):
    if final:
        fnw_ref, o_ref, h_scr = rest
    else:
        o_ref, h_scr = rest
    f = pl.program_id(1)

    @pl.when(f == 0)
    def _():
        shift = mods_ref[0, mod_base:mod_base + 1, :]
        scale = mods_ref[0, mod_base + 1:mod_base + 2, :]
        _norm_mod_store(x_ref, h_scr, nw_ref[...], shift, scale)
        o_ref[...] = jnp.zeros_like(o_ref)

    h = h_scr[...]
    g = jnp.dot(h, wg_ref[...], preferred_element_type=F32)
    u = jnp.dot(h, wu_ref[...], preferred_element_type=F32)
    a = (_silu(g) * u).astype(BF16)
    o_ref[...] += jnp.dot(a, wd_ref[...], preferred_element_type=F32)

    @pl.when(f == pl.num_programs(1) - 1)
    def _():
        gate = mods_ref[0, mod_base + 2:mod_base + 3, :]

        def chunk(rows):
            y = x_ref[rows, :] + 0.5 * gate * o_ref[rows, :]
            if final:
                ms = jnp.mean(y * y, axis=-1, keepdims=True)
                y = y * lax.rsqrt(ms + EPS) * fnw_ref[...]
            o_ref[rows, :] = y

        _row_chunks(o_ref.shape[0], chunk)


def _ffn(x2, mods, nw, wg, wu, wd, fnw, *, mod_base, seq, tm=1024, tf=512):
    m, d = x2.shape
    dff = wg.shape[1]
    tm = _tile(seq, tm)
    tf = _tile(dff, tf)
    tiles_per_batch = seq // tm
    final = fnw is not None
    in_specs = [
        pl.BlockSpec((tm, d), lambda i, f: (i, 0)),
        pl.BlockSpec((1, N_MOD, d), lambda i, f: (i // tiles_per_batch, 0, 0)),
        pl.BlockSpec((1, d), lambda i, f: (0, 0)),
        pl.BlockSpec((d, tf), lambda i, f: (0, f)),
        pl.BlockSpec((d, tf), lambda i, f: (0, f)),
        pl.BlockSpec((tf, d), lambda i, f: (f, 0)),
    ]
    args = [x2, mods, nw, wg, wu, wd]
    if final:
        in_specs.append(pl.BlockSpec((1, d), lambda i, f: (0, 0)))
        args.append(fnw)
    return pl.pallas_call(
        functools.partial(_ffn_kernel, mod_base=mod_base, final=final),
        grid=(m // tm, dff // tf),
        in_specs=in_specs,
        out_specs=pl.BlockSpec((tm, d), lambda i, f: (i, 0)),
        out_shape=jax.ShapeDtypeStruct((m, d), F32),
        scratch_shapes=[pltpu.VMEM((tm, d), BF16)],
        compiler_params=pltpu.CompilerParams(
            dimension_semantics=("parallel", "arbitrary"), vmem_limit_bytes=VMEM_LIMIT),
        name="ffn_final" if final else "ffn",
    )(*args)


def _inproj_kernel(x_ref, mods_ref, nw_ref, w_ref, ws_ref, wdw_ref, bdw_ref, lnw_ref, lnb_ref,
                   *rest, mod_base, tiles_per_batch, taps, rb, n_later):
    later_src = rest[:n_later]
    p_ref, ba_ref, y_ref = rest[n_later:n_later + 3]
    later_dst = rest[n_later + 3:2 * n_later + 3]
    h_scr, ca_s, buf, shf = rest[2 * n_later + 3:]
    i = pl.program_id(0)
    n = pl.program_id(1)
    tm = h_scr.shape[0]
    tq = tm // 4
    nshift = shf.shape[1]
    first = CONV_HALO - (taps - 1)

    @pl.when(n == 0)
    def _():
        shift = mods_ref[0, mod_base:mod_base + 1, :]
        scale = mods_ref[0, mod_base + 1:mod_base + 2, :]
        _norm_mod_store(x_ref, h_scr, nw_ref[...], shift, scale)
        ba_ref[...] = jnp.dot(h_scr[...], ws_ref[...], preferred_element_type=F32)
        ca_s[...] = jnp.dot(h_scr[...], w_ref[...], preferred_element_type=F32)

    @pl.when(n == 1)
    def _():
        @pl.when(i % tiles_per_batch == 0)
        def _():
            buf[0:CONV_HALO, :] = jnp.zeros((CONV_HALO, buf.shape[1]), F32)

        cg = jnp.dot(h_scr[...], w_ref[...], preferred_element_type=F32)
        buf[CONV_HALO:CONV_HALO + tm, :] = ca_s[...] * _sigmoid(cg)
        for src, dst in zip(later_src, later_dst):
            dst[...] = src[...].astype(dst.dtype)

    def conv_rows(r0, r):
        def window(off):
            a, b = divmod(off, HALO)
            if b == 0:
                return buf[r0 + HALO * a + r:r0 + HALO * a + r + rb, :]
            return shf[b - 1, HALO * a + r:HALO * a + r + rb, :]

        acc = wdw_ref[0:1, :] * window(first)
        for tap in range(1, taps):
            acc = acc + wdw_ref[tap:tap + 1, :] * window(first + tap)
        hh = acc + bdw_ref[...]
        mu = jnp.mean(hh, axis=-1, keepdims=True)
        cen = hh - mu
        var = jnp.mean(cen * cen, axis=-1, keepdims=True)
        hn = cen * lax.rsqrt(var + EPS) * lnw_ref[...] + lnb_ref[...]
        y_ref[r0 + r:r0 + r + rb, :] = _silu(hn).astype(y_ref.dtype)

    for qn in range(4):
        @pl.when(n == 2 + qn)
        def _(qn=qn):
            r0 = qn * tq
            for b in range(1, HALO):
                shf[b - 1] = buf[r0 + b:r0 + b + nshift, :]
            p_ref[...] = jnp.dot(h_scr[...], w_ref[...], preferred_element_type=F32)
            for r in range(0, tq, rb):
                conv_rows(r0, r)
            if qn == 3:
                buf[0:CONV_HALO, :] = buf[tm:tm + CONV_HALO, :]


def _inproj(x2, mods, nw, w_main, w_small, w_dw, b_dw, ln_w, ln_b, later, *, mod_base, seq, tm=512,
            rb=32):
    m, d = x2.shape
    taps, cw = w_dw.shape
    assert taps - 1 <= CONV_HALO
    tm = _tile(seq, tm)
    tq = tm // 4
    rb = _tile(tq, rb)
    assert tq % HALO == 0
    tiles_per_batch = seq // tm
    n_tiles = m // tm
    bf16_sublanes = 2 * HALO
    for wl in later:
        assert wl.shape[0] % (n_tiles * bf16_sublanes) == 0, (wl.shape, n_tiles)
    slab = lambda wl: pl.BlockSpec((wl.shape[0] // n_tiles, wl.shape[1]), lambda i, n: (i, 0))
    vec = lambda: pl.BlockSpec((1, cw), lambda i, n: (0, 0))
    return pl.pallas_call(
        functools.partial(_inproj_kernel, mod_base=mod_base, tiles_per_batch=tiles_per_batch,
                          taps=taps, rb=rb, n_later=len(later)),
        grid=(n_tiles, 6),
        in_specs=[
            pl.BlockSpec((tm, d), lambda i, n: (i, 0)),
            pl.BlockSpec((1, N_MOD, d), lambda i, n: (i // tiles_per_batch, 0, 0)),
            pl.BlockSpec((1, d), lambda i, n: (0, 0)),
            pl.BlockSpec((d, cw), lambda i, n: (0, n)),
            pl.BlockSpec((d, LANES), lambda i, n: (0, 0)),
            pl.BlockSpec((taps, cw), lambda i, n: (0, 0)),
            vec(), vec(), vec(),
        ] + [slab(wl) for wl in later],
        out_specs=[
            pl.BlockSpec((tm, cw), lambda i, n: (i, jnp.maximum(n - 2, 0))),
            pl.BlockSpec((tm, LANES), lambda i, n: (i, 0)),
            pl.BlockSpec((tm, cw), lambda i, n: (i, 0)),
        ] + [slab(wl) for wl in later],
        out_shape=[
            jax.ShapeDtypeStruct((m, 4 * cw), F32),
            jax.ShapeDtypeStruct((m, LANES), F32),
            jax.ShapeDtypeStruct((m, cw), BF16),
        ] + [jax.ShapeDtypeStruct(wl.shape, BF16) for wl in later],
        scratch_shapes=[pltpu.VMEM((tm, d), BF16), pltpu.VMEM((tm, cw), F32),
                        pltpu.VMEM((tm + CONV_HALO, cw), F32),
                        pltpu.VMEM((HALO - 1, tq + CONV_HALO - HALO, cw), F32)],
        compiler_params=pltpu.CompilerParams(
            dimension_semantics=("arbitrary", "arbitrary"), vmem_limit_bytes=VMEM_LIMIT),
        name="in_proj",
    )(x2, mods, nw, w_main, w_small, w_dw, b_dw, ln_w, ln_b, *later)


def _dn_kernel(q0_ref, k0_ref, v0_ref, ba0_ref, qn_ref, kn_ref, vn_ref, ban_ref, z_ref, wsh_ref,
               hp_ref, onw_ref, y_ref, halo_s, qkv_s, gate_s, gt_s, state, *, heads, dh, taps,
               npairs):
    t, c = DN_TILE, DN_CHUNK
    ts = npairs * t
    ng = 2 * heads
    w = heads * dh
    j = pl.program_id(1)

    row = lax.broadcasted_iota(jnp.int32, (t, t), 0)
    col = lax.broadcasted_iota(jnp.int32, (t, t), 1)
    same = (row // c) == (col // c)
    tril = same & (row >= col)
    strict = same & (row > col)
    triu = same & (row <= col)
    eye = (row == col).astype(F32)
    blk = lambda n: (row // n) == (col // n)
    base_mask = blk(INV_BASE)
    merge_masks = []
    n = INV_BASE
    while n < c:
        merge_masks.append(blk(2 * n) & jnp.logical_not(blk(n)))
        n *= 2

    def make_tasks(q_src, k_src, v_src, ba_src):
        def gates_task(p):
            def run():
                ba_t = ba_src[0, p * t:(p + 1) * t, :].T[0:ng, :]
                sig_t = _sigmoid(ba_t)
                xx = ba_t + hp_ref[:, 1:2]
                softplus = jnp.maximum(xx, 0.0) + jnp.log1p(jnp.exp(-jnp.abs(xx)))
                gg_t = -jnp.exp(hp_ref[:, 0:1]) * softplus
                g_t = jnp.dot(gg_t, triu.astype(F32), preferred_element_type=F32,
                              precision=lax.Precision.HIGHEST)
                rid = lax.broadcasted_iota(jnp.int32, (ng, t), 0)
                packed = jnp.where(rid < heads, sig_t, g_t)
                gate_s[p * t:(p + 1) * t, :] = jnp.concatenate(
                    [packed, jnp.zeros((LANES - ng, t), F32)], axis=0).T
                gt_s[p] = g_t
            return run

        def conv_task(src, base, h, gain):
            lo = base + h * dh

            def run():
                xs = jnp.concatenate([halo_s[:, lo:lo + dh], src[0, :, h * dh:(h + 1) * dh]],
                                     axis=0)
                acc = wsh_ref[0:1, lo:lo + dh] * xs
                for tap in range(1, taps):
                    acc = pltpu.roll(acc, 1, 0) + wsh_ref[tap:tap + 1, lo:lo + dh] * xs
                y = _silu(acc[HALO:, :])
                if gain is not None:
                    y = y * (lax.rsqrt(jnp.sum(y * y, axis=-1, keepdims=True) + EPS) * gain)
                qkv_s[:, lo:lo + dh] = y
            return run

        def halo_task():
            halo_s[:, 0:w] = q_src[0, ts - HALO:ts, :]
            halo_s[:, w:2 * w] = k_src[0, ts - HALO:ts, :]
            halo_s[:, 2 * w:3 * w] = v_src[0, ts - HALO:ts, :]

        tasks = [gates_task(p) for p in range(npairs)]
        for h in range(heads):
            tasks += [conv_task(q_src, 0, h, dh ** -0.5), conv_task(k_src, w, h, 1.0),
                      conv_task(v_src, 2 * w, h, None)]
        tasks.append(halo_task)
        return tasks

    @pl.when(j == 0)
    def _():
        halo_s[...] = jnp.zeros_like(halo_s)
        state[...] = jnp.zeros_like(state)
        for task in make_tasks(q0_ref, k0_ref, v0_ref, ba0_ref):
            task()

    units = [(p, h) for p in range(npairs) for h in range(heads)]
    us = range(len(units))
    prow = lambda p: slice(p * t, (p + 1) * t)
    q = [qkv_s[prow(p), h * dh:(h + 1) * dh] for p, h in units]
    k = [qkv_s[prow(p), w + h * dh:w + (h + 1) * dh] for p, h in units]
    v = [qkv_s[prow(p), 2 * w + h * dh:2 * w + (h + 1) * dh] for p, h in units]
    gate_cols = [gate_s[prow(p), :] for p in range(npairs)]
    gcum_t = [gt_s[p] for p in range(npairs)]

    tasks = make_tasks(qn_ref, kn_ref, vn_ref, ban_ref)
    n_stages = 11 + 2 * npairs * (t // c)
    per_stage = -(-len(tasks) // n_stages)

    def prep():
        for _ in range(per_stage):
            if tasks:
                tasks.pop(0)()

    beta = [gate_cols[p][:, h:h + 1] for p, h in units]
    g_col = [gate_cols[p][:, heads + h:heads + h + 1] for p, h in units]
    g_row = [gcum_t[p][heads + h:heads + h + 1, :] for p, h in units]
    e_g = [jnp.exp(g) for g in g_col]
    decay = [jnp.exp(jnp.where(tril, g_col[u] - g_row[u], -jnp.inf)) for u in us]
    kb = [k[u] * beta[u] for u in us]
    aq = [lax.dot_general(jnp.concatenate([kb[u], q[u]], axis=0).astype(BF16), k[u].astype(BF16),
                          (((1,), (1,)), ((), ())), preferred_element_type=F32) for u in us]
    prep()
    a_intra = [aq[u][t:] * decay[u] for u in us]
    lmat = [aq[u][:t] * jnp.where(strict, decay[u], 0.0) for u in us]

    xp = [-jnp.where(base_mask, lmat[u], 0.0) for u in us]
    rr = [eye + xp[u] for u in us]
    xp = [_bdot(xp[u], xp[u]) for u in us]
    prep()
    for _ in range(INV_BASE.bit_length() - 3):
        pr = [_bdot(jnp.concatenate([rr[u], xp[u]], axis=0), xp[u]) for u in us]
        prep()
        rr = [rr[u] + pr[u][:t] for u in us]
        xp = [pr[u][t:] for u in us]
    pr = [_bdot(rr[u], xp[u]) for u in us]
    prep()
    rr = [rr[u] + pr[u] for u in us]
    for off_mask in merge_masks:
        pr = [_bdot(rr[u], jnp.where(off_mask, lmat[u], 0.0)) for u in us]
        prep()
        pr = [_bdot(pr[u], rr[u]) for u in us]
        prep()
        rr = [rr[u] - pr[u] for u in us]

    sol = [_bdot(rr[u], jnp.concatenate([v[u] * beta[u], kb[u] * e_g[u]], axis=1)) for u in us]
    prep()
    qg = [q[u] * e_g[u] for u in us]

    hs = range(heads)
    st = [state[h] for h in hs]
    zeros = jnp.zeros((c, dh), F32)
    for p in range(npairs):
        un = [p * heads + h for h in hs]
        for ci in range(t // c):
            rows = slice(ci * c, (ci + 1) * c)
            out_rows = slice(p * t + ci * c, p * t + (ci + 1) * c)
            wq = [_bdot(jnp.concatenate([sol[un[h]][rows, dh:], qg[un[h]][rows]], axis=0), st[h])
                  for h in hs]
            prep()
            v_new = [sol[un[h]][rows, :dh] - wq[h][:c] for h in hs]
            g_last = [g_col[un[h]][ci * c + c - 1:ci * c + c, :] for h in hs]
            kd = [k[un[h]][rows] * jnp.exp(g_last[h] - g_col[un[h]][rows]) for h in hs]
            upd = [lax.dot_general(kd[h].astype(BF16), v_new[h].astype(BF16),
                                   (((0,), (0,)), ((), ())), preferred_element_type=F32) for h in hs]
            prep()
            st = [st[h] * jnp.exp(g_last[h]) + upd[h] for h in hs]
            for h in hs:
                parts = [zeros] * (t // c)
                parts[ci] = v_new[h]
                v_pad = jnp.concatenate(parts, axis=0)
                o = wq[h][c:] + _bdot(a_intra[un[h]][rows], v_pad)
                on = o * lax.rsqrt(jnp.mean(o * o, axis=-1, keepdims=True) + EPS) * onw_ref[...]
                zz = z_ref[0, out_rows, h * dh:(h + 1) * dh]
                y_ref[0, out_rows, h * dh:(h + 1) * dh] = (on * _silu(zz)).astype(y_ref.dtype)
    while tasks:
        tasks.pop(0)()
    for h in hs:
        state[h] = st[h]


def _deltanet(p3, ba3, w_short, hp, onw, *, heads, dh, col0):
    b, s, _ = p3.shape
    w = heads * dh
    taps = w_short.shape[0]
    npairs = DN_PAIRS if s % (DN_PAIRS * DN_TILE) == 0 else 1
    ts = npairs * DN_TILE
    assert s % ts == 0 and col0 % w == 0 and dh == LANES and DN_TILE == LANES
    assert taps - 1 <= HALO
    cb = col0 // w
    nt = s // ts
    nxt = lambda j: jnp.minimum(j + 1, nt - 1)
    tile0 = lambda cc: pl.BlockSpec((1, ts, w), lambda i, j: (i, 0, cc))
    tilen = lambda cc: pl.BlockSpec((1, ts, w), lambda i, j: (i, nxt(j), cc))
    return pl.pallas_call(
        functools.partial(_dn_kernel, heads=heads, dh=dh, taps=taps, npairs=npairs),
        grid=(b, nt),
        in_specs=[
            tile0(cb), tile0(cb + 1), tile0(cb + 2),
            pl.BlockSpec((1, ts, LANES), lambda i, j: (i, 0, 0)),
            tilen(cb), tilen(cb + 1), tilen(cb + 2),
            pl.BlockSpec((1, ts, LANES), lambda i, j: (i, nxt(j), 0)),
            pl.BlockSpec((1, ts, w), lambda i, j: (i, j, cb + 3)),
            pl.BlockSpec((taps, 3 * w), lambda i, j: (0, 0)),
            pl.BlockSpec((2 * heads, 2), lambda i, j: (0, 0)),
            pl.BlockSpec((1, dh), lambda i, j: (0, 0)),
        ],
        out_specs=pl.BlockSpec((1, ts, w), lambda i, j: (i, j, 0)),
        out_shape=jax.ShapeDtypeStruct((b, s, w), BF16),
        scratch_shapes=[pltpu.VMEM((HALO, 3 * w), F32), pltpu.VMEM((ts, 3 * w), F32),
                        pltpu.VMEM((ts, LANES), F32), pltpu.VMEM((npairs, 2 * heads, DN_TILE), F32),
                        pltpu.VMEM((heads, dh, dh), F32)],
        compiler_params=pltpu.CompilerParams(
            dimension_semantics=("parallel", "arbitrary"), vmem_limit_bytes=VMEM_LIMIT),
        name="deltanet",
    )(p3, p3, p3, ba3, p3, p3, p3, ba3, p3, w_short, hp, onw)


def _outproj_kernel(x_ref, mods_ref, yc_ref, yd_ref, wc_ref, wd_ref, o_ref, *, mod_idx):
    y = jnp.dot(yc_ref[...], wc_ref[...], preferred_element_type=F32)
    y = y + jnp.dot(yd_ref[...], wd_ref[...], preferred_element_type=F32)
    o_ref[...] = x_ref[...] + mods_ref[0, mod_idx:mod_idx + 1, :] * y


def _outproj(x2, mods, yc, yd, w_o, *, mod_idx, seq, tm=512):
    m, d = x2.shape
    half = w_o.shape[0] // 2
    assert yc.shape[1] == half and yd.shape[1] == half
    tm = _tile(seq, tm)
    tiles_per_batch = seq // tm
    return pl.pallas_call(
        functools.partial(_outproj_kernel, mod_idx=mod_idx),
        grid=(m // tm,),
        in_specs=[
            pl.BlockSpec((tm, d), lambda i: (i, 0)),
            pl.BlockSpec((1, N_MOD, d), lambda i: (i // tiles_per_batch, 0, 0)),
            pl.BlockSpec((tm, yc.shape[1]), lambda i: (i, 0)),
            pl.BlockSpec((tm, yd.shape[1]), lambda i: (i, 0)),
            pl.BlockSpec((half, d), lambda i: (0, 0)),
            pl.BlockSpec((half, d), lambda i: (1, 0)),
        ],
        out_specs=pl.BlockSpec((tm, d), lambda i: (i, 0)),
        out_shape=jax.ShapeDtypeStruct((m, d), F32),
        compiler_params=pltpu.CompilerParams(
            dimension_semantics=("parallel",), vmem_limit_bytes=VMEM_LIMIT),
        name="out_proj",
    )(x2, mods, yc, yd, w_o, w_o)


def kernel(x, c, w_ada, b_ada, ffn1_norm, ffn1_wg, ffn1_wu, ffn1_wd, mix_norm, w_in, w_dw, b_dw,
           conv_ln_w, conv_ln_b, w_short, a_log, dt_bias, dn_norm_w, w_out, ffn2_norm, ffn2_wg,
           ffn2_wu, ffn2_wd, final_norm):
    b, s, d = x.shape
    depth = w_ada.shape[0]
    heads = a_log.shape[1]
    dh = dn_norm_w.shape[1]
    cw = w_dw.shape[2]
    dnw = heads * dh
    n_main = 2 * cw + 4 * dnw
    assert w_in.shape[2] == n_main + 2 * heads and 2 * heads <= LANES and cw == dnw and depth >= 1

    x2 = x.reshape(b * s, d)
    c_pad = jnp.pad(c, ((0, (-b) % HALO), (0, 0)))
    row = lambda v: v.reshape(1, -1)
    for l in range(depth):
        mods = _mods(c_pad, w_ada[l], row(b_ada[l]))[:b].reshape(b, N_MOD, d)
        last = l == depth - 1

        x2 = _ffn(x2, mods, row(ffn1_norm[l]), ffn1_wg[l].astype(BF16), ffn1_wu[l].astype(BF16),
                  ffn1_wd[l].astype(BF16), None, mod_base=0, seq=s)

        w_main = w_in[l].astype(BF16)
        w_small = jnp.pad(w_in[l][:, n_main:], ((0, 0), (0, LANES - 2 * heads))).astype(BF16)
        later = [ffn2_wg[l], ffn2_wu[l], ffn2_wd[l], w_out[l]]
        p, ba, y_conv, wg2, wu2, wd2, wo = _inproj(
            x2, mods, row(mix_norm[l]), w_main, w_small, w_dw[l], row(b_dw[l]), row(conv_ln_w[l]),
            row(conv_ln_b[l]), later, mod_base=3, seq=s)
        p3 = p.reshape(b, s, 4 * dnw)
        ba3 = ba.reshape(b, s, LANES)

        zh = jnp.zeros((heads,), F32)
        hp = jnp.stack([jnp.concatenate([zh, a_log[l]]), jnp.concatenate([zh, dt_bias[l]])], axis=1)
        y_dn = _deltanet(p3, ba3, w_short[l], hp, row(dn_norm_w[l]), heads=heads, dh=dh, col0=0)

        x2 = _outproj(x2, mods, y_conv, y_dn.reshape(b * s, dnw),
                      wo, mod_idx=5, seq=s)

        x2 = _ffn(x2, mods, row(ffn2_norm[l]), wg2, wu2, wd2,
                  row(final_norm) if last else None, mod_base=6, seq=s)
    return x2.reshape(b, s, d)
```

```python
import functools

import jax
import jax.numpy as jnp
from jax import lax
from jax.experimental import pallas as pl
from jax.experimental.pallas import tpu as pltpu

F32 = jnp.float32
BF16 = jnp.bfloat16

EPS = 1e-6
N_MOD = 9
DN_CHUNK = 64
DN_TILE = 2 * DN_CHUNK
DN_PAIRS = 2
INV_BASE = 8
LANES = 128
HALO = 8
ROW_CHUNK = 32
ROW_UNROLL = 4
CONV_HALO = 32
VMEM_LIMIT = 60 * 1024 * 1024


def _tile(n, pref):
    t = min(n, pref)
    assert n % t == 0, (n, pref)
    return t


def _sigmoid(v):
    return jax.nn.sigmoid(v)


def _silu(v):
    return v * _sigmoid(v)


def _bdot(a, b):
    return jnp.dot(a.astype(BF16), b.astype(BF16), preferred_element_type=F32)


def _row_chunks(n_rows, fn):
    rc = min(ROW_CHUNK, n_rows)
    assert n_rows % rc == 0

    def body(i, carry):
        fn(pl.ds(pl.multiple_of(i * rc, rc), rc))
        return carry

    trips = n_rows // rc
    lax.fori_loop(0, trips, body, 0, unroll=min(ROW_UNROLL, trips))


def _norm_mod_store(x_ref, h_ref, nw, shift, scale, zero_ref=None):
    gain = nw * (1.0 + scale)

    def chunk(rows):
        x = x_ref[rows, :]
        ms = jnp.mean(x * x, axis=-1, keepdims=True)
        h_ref[rows, :] = (x * lax.rsqrt(ms + EPS) * gain + shift).astype(h_ref.dtype)
        if zero_ref is not None:
            zero_ref[rows, :] = jnp.zeros_like(x)

    _row_chunks(x_ref.shape[0], chunk)


def _mods_kernel(c_ref, w_ref, b_ref, o_ref):
    s = _silu(c_ref[...])
    o_ref[...] = _bdot(s, w_ref[...]) + b_ref[...]


def _mods(c_pad, w_ada, b_ada):
    rows, d = c_pad.shape
    n = w_ada.shape[1]
    tn = _tile(d, 1024)
    assert n % tn == 0
    return pl.pallas_call(
        _mods_kernel,
        grid=(n // tn,),
        in_specs=[
            pl.BlockSpec((rows, d), lambda j: (0, 0)),
            pl.BlockSpec((d, tn), lambda j: (0, j)),
            pl.BlockSpec((1, tn), lambda j: (0, j)),
        ],
        out_specs=pl.BlockSpec((rows, tn), lambda j: (0, j)),
        out_shape=jax.ShapeDtypeStruct((rows, n), F32),
        compiler_params=pltpu.CompilerParams(
            dimension_semantics=("parallel",), vmem_limit_bytes=VMEM_LIMIT),
        name="mods",
    )(c_pad, w_ada, b_ada)


def _ffn_kernel(x_ref, mods_ref, nw_ref, wg_ref, wu_ref, wd_ref, *rest, mod_base, final):
    if final:
        fnw_ref, o_ref, h_scr = rest
    else:
        o_ref, h_scr = rest
    f = pl.program_id(1)

    @pl.when(f == 0)
    def _():
        shift = mods_ref[0, mod_base:mod_base + 1, :]
        scale = mods_ref[0, mod_base + 1:mod_base + 2, :]
        _norm_mod_store(x_ref, h_scr, nw_ref[...], shift, scale, zero_ref=o_ref)

    h = h_scr[...]
    g = jnp.dot(h, wg_ref[...], preferred_element_type=F32)
    u = jnp.dot(h, wu_ref[...], preferred_element_type=F32)
    a = (_silu(g) * u).astype(BF16)
    o_ref[...] += jnp.dot(a, wd_ref[...], preferred_element_type=F32)

    @pl.when(f == pl.num_programs(1) - 1)
    def _():
        gate = mods_ref[0, mod_base + 2:mod_base + 3, :]

        def chunk(rows):
            y = x_ref[rows, :] + 0.5 * gate * o_ref[rows, :]
            if final:
                ms = jnp.mean(y * y, axis=-1, keepdims=True)
                y = y * lax.rsqrt(ms + EPS) * fnw_ref[...]
            o_ref[rows, :] = y

        _row_chunks(o_ref.shape[0], chunk)


def _ffn(x2, mods, nw, wg, wu, wd, fnw, *, mod_base, seq, tm=1024, tf=512):
    m, d = x2.shape
    dff = wg.shape[1]
    tm = _tile(seq, tm)
    tf = _tile(dff, tf)
    tiles_per_batch = seq // tm
    final = fnw is not None
    in_specs = [
        pl.BlockSpec((tm, d), lambda i, f: (i, 0)),
        pl.BlockSpec((1, N_MOD, d), lambda i, f: (i // tiles_per_batch, 0, 0)),
        pl.BlockSpec((1, d), lambda i, f: (0, 0)),
        pl.BlockSpec((d, tf), lambda i, f: (0, f)),
        pl.BlockSpec((d, tf), lambda i, f: (0, f)),
        pl.BlockSpec((tf, d), lambda i, f: (f, 0)),
    ]
    args = [x2, mods, nw, wg, wu, wd]
    if final:
        in_specs.append(pl.BlockSpec((1, d), lambda i, f: (0, 0)))
        args.append(fnw)
    return pl.pallas_call(
        functools.partial(_ffn_kernel, mod_base=mod_base, final=final),
        grid=(m // tm, dff // tf),
        in_specs=in_specs,
        out_specs=pl.BlockSpec((tm, d), lambda i, f: (i, 0)),
        out_shape=jax.ShapeDtypeStruct((m, d), F32),
        scratch_shapes=[pltpu.VMEM((tm, d), BF16)],
        compiler_params=pltpu.CompilerParams(
            dimension_semantics=("parallel", "arbitrary"), vmem_limit_bytes=VMEM_LIMIT),
        name="ffn_final" if final else "ffn",
    )(*args)


def _inproj_kernel(x_ref, mods_ref, nw_ref, w_ref, ws_ref, wdw_ref, bdw_ref, lnw_ref, lnb_ref,
                   *rest, mod_base, tiles_per_batch, taps, rb, n_later):
    later_src = rest[:n_later]
    p_ref, ba_ref, y_ref = rest[n_later:n_later + 3]
    later_dst = rest[n_later + 3:2 * n_later + 3]
    h_scr, ca_s, buf, shf = rest[2 * n_later + 3:]
    i = pl.program_id(0)
    n = pl.program_id(1)
    tm = h_scr.shape[0]
    tq = tm // 4
    nshift = shf.shape[1]
    first = CONV_HALO - (taps - 1)

    def cast_later(k):
        if k < n_later:
            later_dst[k][...] = later_src[k][...].astype(later_dst[k].dtype)

    @pl.when(n == 0)
    def _():
        shift = mods_ref[0, mod_base:mod_base + 1, :]
        scale = mods_ref[0, mod_base + 1:mod_base + 2, :]
        _norm_mod_store(x_ref, h_scr, nw_ref[...], shift, scale)
        ba_ref[...] = jnp.dot(h_scr[...], ws_ref[...], preferred_element_type=F32)
        ca_s[...] = jnp.dot(h_scr[...], w_ref[...], preferred_element_type=F32)

    @pl.when(n == 1)
    def _():
        @pl.when(i % tiles_per_batch == 0)
        def _():
            buf[0:CONV_HALO, :] = jnp.zeros((CONV_HALO, buf.shape[1]), F32)

        cg = jnp.dot(h_scr[...], w_ref[...], preferred_element_type=F32)
        buf[CONV_HALO:CONV_HALO + tm, :] = ca_s[...] * _sigmoid(cg)
        cast_later(0)

    def conv_rows(r0, r):
        def window(off):
            a, b = divmod(off, HALO)
            if b == 0:
                return buf[r0 + HALO * a + r:r0 + HALO * a + r + rb, :]
            return shf[b - 1, HALO * a + r:HALO * a + r + rb, :]

        acc = wdw_ref[0:1, :] * window(first)
        for tap in range(1, taps):
            acc = acc + wdw_ref[tap:tap + 1, :] * window(first + tap)
        hh = acc + bdw_ref[...]
        mu = jnp.mean(hh, axis=-1, keepdims=True)
        cen = hh - mu
        var = jnp.mean(cen * cen, axis=-1, keepdims=True)
        hn = cen * lax.rsqrt(var + EPS) * lnw_ref[...] + lnb_ref[...]
        y_ref[r0 + r:r0 + r + rb, :] = _silu(hn).astype(y_ref.dtype)

    for qn in range(4):
        @pl.when(n == 2 + qn)
        def _(qn=qn):
            r0 = qn * tq
            for b in range(1, HALO):
                shf[b - 1] = buf[r0 + b:r0 + b + nshift, :]
            p_ref[...] = jnp.dot(h_scr[...], w_ref[...], preferred_element_type=F32)
            for r in range(0, tq, rb):
                conv_rows(r0, r)
            cast_later(qn + 1)
            if qn == 3:
                buf[0:CONV_HALO, :] = buf[tm:tm + CONV_HALO, :]


def _inproj(x2, mods, nw, w_main, w_small, w_dw, b_dw, ln_w, ln_b, later, *, mod_base, seq, tm=512,
            rb=32):
    m, d = x2.shape
    taps, cw = w_dw.shape
    assert taps - 1 <= CONV_HALO
    tm = _tile(seq, tm)
    tq = tm // 4
    rb = _tile(tq, rb)
    assert tq % HALO == 0
    tiles_per_batch = seq // tm
    n_tiles = m // tm
    bf16_sublanes = 2 * HALO
    for wl in later:
        assert wl.shape[0] % (n_tiles * bf16_sublanes) == 0, (wl.shape, n_tiles)
    assert len(later) <= 5

    def slab(k):
        rows = later[k].shape[0] // n_tiles
        return pl.BlockSpec((rows, later[k].shape[1]),
                            lambda i, n: (jnp.where(n >= 1 + k, i, jnp.maximum(i - 1, 0)), 0))
    vec = lambda: pl.BlockSpec((1, cw), lambda i, n: (0, 0))
    return pl.pallas_call(
        functools.partial(_inproj_kernel, mod_base=mod_base, tiles_per_batch=tiles_per_batch,
                          taps=taps, rb=rb, n_later=len(later)),
        grid=(n_tiles, 6),
        in_specs=[
            pl.BlockSpec((tm, d), lambda i, n: (i, 0)),
            pl.BlockSpec((1, N_MOD, d), lambda i, n: (i // tiles_per_batch, 0, 0)),
            pl.BlockSpec((1, d), lambda i, n: (0, 0)),
            pl.BlockSpec((d, cw), lambda i, n: (0, n)),
            pl.BlockSpec((d, LANES), lambda i, n: (0, 0)),
            pl.BlockSpec((taps, cw), lambda i, n: (0, 0)),
            vec(), vec(), vec(),
        ] + [slab(k) for k in range(len(later))],
        out_specs=[
            pl.BlockSpec((tm, cw), lambda i, n: (i, jnp.maximum(n - 2, 0))),
            pl.BlockSpec((tm, LANES), lambda i, n: (i, 0)),
            pl.BlockSpec((tm, cw), lambda i, n: (i, 0)),
        ] + [slab(k) for k in range(len(later))],
        out_shape=[
            jax.ShapeDtypeStruct((m, 4 * cw), F32),
            jax.ShapeDtypeStruct((m, LANES), F32),
            jax.ShapeDtypeStruct((m, cw), BF16),
        ] + [jax.ShapeDtypeStruct(wl.shape, BF16) for wl in later],
        scratch_shapes=[pltpu.VMEM((tm, d), BF16), pltpu.VMEM((tm, cw), F32),
                        pltpu.VMEM((tm + CONV_HALO, cw), F32),
                        pltpu.VMEM((HALO - 1, tq + CONV_HALO - HALO, cw), F32)],
        compiler_params=pltpu.CompilerParams(
            dimension_semantics=("arbitrary", "arbitrary"), vmem_limit_bytes=VMEM_LIMIT),
        name="in_proj",
    )(x2, mods, nw, w_main, w_small, w_dw, b_dw, ln_w, ln_b, *later)


def _dn_kernel(q0_ref, k0_ref, v0_ref, ba0_ref, qn_ref, kn_ref, vn_ref, ban_ref, z_ref, wsh_ref,
               hp_ref, onw_ref, y_ref, halo_s, qkv_s, gate_s, gt_s, state, *, heads, dh, taps,
               npairs):
    t, c = DN_TILE, DN_CHUNK
    ts = npairs * t
    ng = 2 * heads
    w = heads * dh
    j = pl.program_id(1)

    row = lax.broadcasted_iota(jnp.int32, (t, t), 0)
    col = lax.broadcasted_iota(jnp.int32, (t, t), 1)
    same = (row // c) == (col // c)
    tril = same & (row >= col)
    strict = same & (row > col)
    triu = same & (row <= col)
    eye = (row == col).astype(F32)
    blk = lambda n: (row // n) == (col // n)
    base_mask = blk(INV_BASE)
    merge_masks = []
    n = INV_BASE
    while n < c:
        merge_masks.append(blk(2 * n) & jnp.logical_not(blk(n)))
        n *= 2

    def make_tasks(q_src, k_src, v_src, ba_src):
        def gates_task(p):
            def run():
                ba_t = ba_src[0, p * t:(p + 1) * t, :].T[0:ng, :]
                sig_t = _sigmoid(ba_t)
                xx = ba_t + hp_ref[:, 1:2]
                softplus = jnp.maximum(xx, 0.0) + jnp.log1p(jnp.exp(-jnp.abs(xx)))
                gg_t = -jnp.exp(hp_ref[:, 0:1]) * softplus
                g_t = jnp.dot(gg_t, triu.astype(F32), preferred_element_type=F32,
                              precision=lax.Precision.HIGHEST)
                rid = lax.broadcasted_iota(jnp.int32, (ng, t), 0)
                packed = jnp.where(rid < heads, sig_t, g_t)
                gate_s[p * t:(p + 1) * t, :] = jnp.concatenate(
                    [packed, jnp.zeros((LANES - ng, t), F32)], axis=0).T
                gt_s[p] = g_t
            return run

        def conv_task(src, base, h, gain):
            lo = base + h * dh

            def run():
                xs = jnp.concatenate([halo_s[:, lo:lo + dh], src[0, :, h * dh:(h + 1) * dh]],
                                     axis=0)
                acc = wsh_ref[0:1, lo:lo + dh] * xs
                for tap in range(1, taps):
                    acc = pltpu.roll(acc, 1, 0) + wsh_ref[tap:tap + 1, lo:lo + dh] * xs
                y = _silu(acc[HALO:, :])
                if gain is not None:
                    y = y * (lax.rsqrt(jnp.sum(y * y, axis=-1, keepdims=True) + EPS) * gain)
                qkv_s[:, lo:lo + dh] = y
            return run

        def halo_task():
            halo_s[:, 0:w] = q_src[0, ts - HALO:ts, :]
            halo_s[:, w:2 * w] = k_src[0, ts - HALO:ts, :]
            halo_s[:, 2 * w:3 * w] = v_src[0, ts - HALO:ts, :]

        tasks = [gates_task(p) for p in range(npairs)]
        for h in range(heads):
            tasks += [conv_task(q_src, 0, h, dh ** -0.5), conv_task(k_src, w, h, 1.0),
                      conv_task(v_src, 2 * w, h, None)]
        tasks.append(halo_task)
        return tasks

    @pl.when(j == 0)
    def _():
        halo_s[...] = jnp.zeros_like(halo_s)
        state[...] = jnp.zeros_like(state)
        for task in make_tasks(q0_ref, k0_ref, v0_ref, ba0_ref):
            task()

    units = [(p, h) for p in range(npairs) for h in range(heads)]
    us = range(len(units))
    prow = lambda p: slice(p * t, (p + 1) * t)
    q = [qkv_s[prow(p), h * dh:(h + 1) * dh] for p, h in units]
    k = [qkv_s[prow(p), w + h * dh:w + (h + 1) * dh] for p, h in units]
    v = [qkv_s[prow(p), 2 * w + h * dh:2 * w + (h + 1) * dh] for p, h in units]
    gate_cols = [gate_s[prow(p), :] for p in range(npairs)]
    gcum_t = [gt_s[p] for p in range(npairs)]

    tasks = make_tasks(qn_ref, kn_ref, vn_ref, ban_ref)
    n_stages = 11 + 2 * npairs * (t // c)
    per_stage = -(-len(tasks) // n_stages)

    def prep():
        for _ in range(per_stage):
            if tasks:
                tasks.pop(0)()

    beta = [gate_cols[p][:, h:h + 1] for p, h in units]
    g_col = [gate_cols[p][:, heads + h:heads + h + 1] for p, h in units]
    g_row = [gcum_t[p][heads + h:heads + h + 1, :] for p, h in units]
    e_g = [jnp.exp(g) for g in g_col]
    decay = [jnp.exp(jnp.where(tril, g_col[u] - g_row[u], -jnp.inf)) for u in us]
    kb = [k[u] * beta[u] for u in us]
    aq = [lax.dot_general(jnp.concatenate([kb[u], q[u]], axis=0).astype(BF16), k[u].astype(BF16),
                          (((1,), (1,)), ((), ())), preferred_element_type=F32) for u in us]
    prep()
    a_intra = [aq[u][t:] * decay[u] for u in us]
    lmat = [aq[u][:t] * jnp.where(strict, decay[u], 0.0) for u in us]

    xp = [-jnp.where(base_mask, lmat[u], 0.0) for u in us]
    rr = [eye + xp[u] for u in us]
    xp = [_bdot(xp[u], xp[u]) for u in us]
    prep()
    for _ in range(INV_BASE.bit_length() - 3):
        pr = [_bdot(jnp.concatenate([rr[u], xp[u]], axis=0), xp[u]) for u in us]
        prep()
        rr = [rr[u] + pr[u][:t] for u in us]
        xp = [pr[u][t:] for u in us]
    pr = [_bdot(rr[u], xp[u]) for u in us]
    prep()
    rr = [rr[u] + pr[u] for u in us]
    for off_mask in merge_masks:
        pr = [_bdot(rr[u], jnp.where(off_mask, lmat[u], 0.0)) for u in us]
        prep()
        pr = [_bdot(pr[u], rr[u]) for u in us]
        prep()
        rr = [rr[u] - pr[u] for u in us]

    sol = [_bdot(rr[u], jnp.concatenate([v[u] * beta[u], kb[u] * e_g[u]], axis=1)) for u in us]
    prep()
    qg = [q[u] * e_g[u] for u in us]

    hs = range(heads)
    st = [state[h] for h in hs]
    zeros = jnp.zeros((c, dh), F32)
    for p in range(npairs):
        un = [p * heads + h for h in hs]
        for ci in range(t // c):
            rows = slice(ci * c, (ci + 1) * c)
            out_rows = slice(p * t + ci * c, p * t + (ci + 1) * c)
            wq = [_bdot(jnp.concatenate([sol[un[h]][rows, dh:], qg[un[h]][rows]], axis=0), st[h])
                  for h in hs]
            prep()
            v_new = [sol[un[h]][rows, :dh] - wq[h][:c] for h in hs]
            g_last = [g_col[un[h]][ci * c + c - 1:ci * c + c, :] for h in hs]
            kd = [k[un[h]][rows] * jnp.exp(g_last[h] - g_col[un[h]][rows]) for h in hs]
            upd = [lax.dot_general(kd[h].astype(BF16), v_new[h].astype(BF16),
                                   (((0,), (0,)), ((), ())), preferred_element_type=F32) for h in hs]
            prep()
            st = [st[h] * jnp.exp(g_last[h]) + upd[h] for h in hs]
            for h in hs:
                parts = [zeros] * (t // c)
                parts[ci] = v_new[h]
                v_pad = jnp.concatenate(parts, axis=0)
                o = wq[h][c:] + _bdot(a_intra[un[h]][rows], v_pad)
                on = o * lax.rsqrt(jnp.mean(o * o, axis=-1, keepdims=True) + EPS) * onw_ref[...]
                zz = z_ref[0, out_rows, h * dh:(h + 1) * dh]
                y_ref[0, out_rows, h * dh:(h + 1) * dh] = (on * _silu(zz)).astype(y_ref.dtype)
    while tasks:
        tasks.pop(0)()
    for h in hs:
        state[h] = st[h]


def _deltanet(p3, ba3, w_short, hp, onw, *, heads, dh, col0):
    b, s, _ = p3.shape
    w = heads * dh
    taps = w_short.shape[0]
    npairs = DN_PAIRS if s % (DN_PAIRS * DN_TILE) == 0 else 1
    ts = npairs * DN_TILE
    assert s % ts == 0 and col0 % w == 0 and dh == LANES and DN_TILE == LANES
    assert taps - 1 <= HALO
    cb = col0 // w
    nt = s // ts
    nxt = lambda j: jnp.minimum(j + 1, nt - 1)
    tile0 = lambda cc: pl.BlockSpec((1, ts, w), lambda i, j: (i, 0, cc))
    tilen = lambda cc: pl.BlockSpec((1, ts, w), lambda i, j: (i, nxt(j), cc))
    return pl.pallas_call(
        functools.partial(_dn_kernel, heads=heads, dh=dh, taps=taps, npairs=npairs),
        grid=(b, nt),
        in_specs=[
            tile0(cb), tile0(cb + 1), tile0(cb + 2),
            pl.BlockSpec((1, ts, LANES), lambda i, j: (i, 0, 0)),
            tilen(cb), tilen(cb + 1), tilen(cb + 2),
            pl.BlockSpec((1, ts, LANES), lambda i, j: (i, nxt(j), 0)),
            pl.BlockSpec((1, ts, w), lambda i, j: (i, j, cb + 3)),
            pl.BlockSpec((taps, 3 * w), lambda i, j: (0, 0)),
            pl.BlockSpec((2 * heads, 2), lambda i, j: (0, 0)),
            pl.BlockSpec((1, dh), lambda i, j: (0, 0)),
        ],
        out_specs=pl.BlockSpec((1, ts, w), lambda i, j: (i, j, 0)),
        out_shape=jax.ShapeDtypeStruct((b, s, w), BF16),
        scratch_shapes=[pltpu.VMEM((HALO, 3 * w), F32), pltpu.VMEM((ts, 3 * w), F32),
                        pltpu.VMEM((ts, LANES), F32), pltpu.VMEM((npairs, 2 * heads, DN_TILE), F32),
                        pltpu.VMEM((heads, dh, dh), F32)],
        compiler_params=pltpu.CompilerParams(
            dimension_semantics=("parallel", "arbitrary"), vmem_limit_bytes=VMEM_LIMIT),
        name="deltanet",
    )(p3, p3, p3, ba3, p3, p3, p3, ba3, p3, w_short, hp, onw)


def _outproj_kernel(x_ref, mods_ref, yc_ref, yd_ref, wc_ref, wd_ref, o_ref, *, mod_idx):
    y = jnp.dot(yc_ref[...], wc_ref[...], preferred_element_type=F32)
    y = y + jnp.dot(yd_ref[...], wd_ref[...], preferred_element_type=F32)
    o_ref[...] = x_ref[...] + mods_ref[0, mod_idx:mod_idx + 1, :] * y


def _outproj(x2, mods, yc, yd, w_o, *, mod_idx, seq, tm=512):
    m, d = x2.shape
    half = w_o.shape[0] // 2
    assert yc.shape[1] == half and yd.shape[1] == half
    tm = _tile(seq, tm)
    tiles_per_batch = seq // tm
    return pl.pallas_call(
        functools.partial(_outproj_kernel, mod_idx=mod_idx),
        grid=(m // tm,),
        in_specs=[
            pl.BlockSpec((tm, d), lambda i: (i, 0)),
            pl.BlockSpec((1, N_MOD, d), lambda i: (i // tiles_per_batch, 0, 0)),
            pl.BlockSpec((tm, yc.shape[1]), lambda i: (i, 0)),
            pl.BlockSpec((tm, yd.shape[1]), lambda i: (i, 0)),
            pl.BlockSpec((half, d), lambda i: (0, 0)),
            pl.BlockSpec((half, d), lambda i: (1, 0)),
        ],
        out_specs=pl.BlockSpec((tm, d), lambda i: (i, 0)),
        out_shape=jax.ShapeDtypeStruct((m, d), F32),
        compiler_params=pltpu.CompilerParams(
            dimension_semantics=("parallel",), vmem_limit_bytes=VMEM_LIMIT),
        name="out_proj",
    )(x2, mods, yc, yd, w_o, w_o)


def kernel(x, c, w_ada, b_ada, ffn1_norm, ffn1_wg, ffn1_wu, ffn1_wd, mix_norm, w_in, w_dw, b_dw,
           conv_ln_w, conv_ln_b, w_short, a_log, dt_bias, dn_norm_w, w_out, ffn2_norm, ffn2_wg,
           ffn2_wu, ffn2_wd, final_norm):
    b, s, d = x.shape
    depth = w_ada.shape[0]
    heads = a_log.shape[1]
    dh = dn_norm_w.shape[1]
    cw = w_dw.shape[2]
    dnw = heads * dh
    n_main = 2 * cw + 4 * dnw
    assert w_in.shape[2] == n_main + 2 * heads and 2 * heads <= LANES and cw == dnw and depth >= 1

    x2 = x.reshape(b * s, d)
    c_pad = jnp.pad(c, ((0, (-b) % HALO), (0, 0)))
    row = lambda v: v.reshape(1, -1)
    for l in range(depth):
        mods = _mods(c_pad, w_ada[l], row(b_ada[l]))[:b].reshape(b, N_MOD, d)
        last = l == depth - 1

        x2 = _ffn(x2, mods, row(ffn1_norm[l]), ffn1_wg[l].astype(BF16), ffn1_wu[l].astype(BF16),
                  ffn1_wd[l].astype(BF16), None, mod_base=0, seq=s)

        w_main = w_in[l].astype(BF16)
        w_small = jnp.pad(w_in[l][:, n_main:], ((0, 0), (0, LANES - 2 * heads))).astype(BF16)
        later = [ffn2_wg[l], ffn2_wu[l], ffn2_wd[l], w_out[l]]
        p, ba, y_conv, wg2, wu2, wd2, wo = _inproj(
            x2, mods, row(mix_norm[l]), w_main, w_small, w_dw[l], row(b_dw[l]), row(conv_ln_w[l]),
            row(conv_ln_b[l]), later, mod_base=3, seq=s)
        p3 = p.reshape(b, s, 4 * dnw)
        ba3 = ba.reshape(b, s, LANES)

        zh = jnp.zeros((heads,), F32)
        hp = jnp.stack([jnp.concatenate([zh, a_log[l]]), jnp.concatenate([zh, dt_bias[l]])], axis=1)
        y_dn = _deltanet(p3, ba3, w_short[l], hp, row(dn_norm_w[l]), heads=heads, dh=dh, col0=0)

        x2 = _outproj(x2, mods, y_conv, y_dn.reshape(b * s, dnw),
                      wo, mod_idx=5, seq=s)

        x2 = _ffn(x2, mods, row(ffn2_norm[l]), wg2, wu2, wd2,
                  row(final_norm) if last else None, mod_base=6, seq=s)
    return x2.reshape(b, s, d)
```

```python
import functools

import jax
import jax.numpy as jnp
from jax import lax
from jax.experimental import pallas as pl
from jax.experimental.pallas import tpu as pltpu

F32 = jnp.float32
BF16 = jnp.bfloat16

EPS = 1e-6
N_MOD = 9
DN_CHUNK = 64
DN_TILE = 2 * DN_CHUNK
DN_PAIRS = 2
INV_BASE = 8
LANES = 128
HALO = 8
ROW_CHUNK = 32
ROW_UNROLL = 4
FINAL_ROW_CHUNK = 64
CONV_HALO = 32
VMEM_LIMIT = 60 * 1024 * 1024


def _tile(n, pref):
    t = min(n, pref)
    assert n % t == 0, (n, pref)
    return t


def _sigmoid(v):
    return jax.nn.sigmoid(v)


def _silu(v):
    return v * _sigmoid(v)


def _bdot(a, b):
    return jnp.dot(a.astype(BF16), b.astype(BF16), preferred_element_type=F32)


def _row_chunks(n_rows, fn, rows_per_chunk=ROW_CHUNK):
    rc = min(rows_per_chunk, n_rows)
    assert n_rows % rc == 0

    def body(i, carry):
        fn(pl.ds(pl.multiple_of(i * rc, rc), rc))
        return carry

    trips = n_rows // rc
    lax.fori_loop(0, trips, body, 0, unroll=min(ROW_UNROLL, trips))


def _norm_mod_store(x_ref, h_ref, nw, shift, scale, zero_ref=None):
    gain = nw * (1.0 + scale)

    def chunk(rows):
        x = x_ref[rows, :]
        ms = jnp.mean(x * x, axis=-1, keepdims=True)
        h_ref[rows, :] = (x * lax.rsqrt(ms + EPS) * gain + shift).astype(h_ref.dtype)
        if zero_ref is not None:
            zero_ref[rows, :] = jnp.zeros_like(x)

    _row_chunks(x_ref.shape[0], chunk)


def _mods_kernel(c_ref, w_ref, b_ref, o_ref):
    s = _silu(c_ref[...])
    o_ref[...] = _bdot(s, w_ref[...]) + b_ref[...]


def _mods(c_pad, w_ada, b_ada):
    rows, d = c_pad.shape
    n = w_ada.shape[1]
    tn = _tile(d, 1024)
    assert n % tn == 0
    return pl.pallas_call(
        _mods_kernel,
        grid=(n // tn,),
        in_specs=[
            pl.BlockSpec((rows, d), lambda j: (0, 0)),
            pl.BlockSpec((d, tn), lambda j: (0, j)),
            pl.BlockSpec((1, tn), lambda j: (0, j)),
        ],
        out_specs=pl.BlockSpec((rows, tn), lambda j: (0, j)),
        out_shape=jax.ShapeDtypeStruct((rows, n), F32),
        compiler_params=pltpu.CompilerParams(
            dimension_semantics=("parallel",), vmem_limit_bytes=VMEM_LIMIT),
        name="mods",
    )(c_pad, w_ada, b_ada)


def _ffn_kernel(x_ref, mods_ref, nw_ref, wg_ref, wu_ref, wd_ref, *rest, mod_base, final):
    if final:
        fnw_ref, o_ref, h_scr = rest
    else:
        o_ref, h_scr = rest
    f = pl.program_id(1)

    @pl.when(f == 0)
    def _():
        shift = mods_ref[0, mod_base:mod_base + 1, :]
        scale = mods_ref[0, mod_base + 1:mod_base + 2, :]
        _norm_mod_store(x_ref, h_scr, nw_ref[...], shift, scale, zero_ref=o_ref)

    h = h_scr[...]
    g = jnp.dot(h, wg_ref[...], preferred_element_type=F32)
    u = jnp.dot(h, wu_ref[...], preferred_element_type=F32)
    a = (_silu(g) * u).astype(BF16)
    o_ref[...] += jnp.dot(a, wd_ref[...], preferred_element_type=F32)

    @pl.when(f == pl.num_programs(1) - 1)
    def _():
        gate = mods_ref[0, mod_base + 2:mod_base + 3, :]

        def chunk(rows):
            y = x_ref[rows, :] + 0.5 * gate * o_ref[rows, :]
            if final:
                ms = jnp.mean(y * y, axis=-1, keepdims=True)
                y = y * lax.rsqrt(ms + EPS) * fnw_ref[...]
            o_ref[rows, :] = y

        _row_chunks(o_ref.shape[0], chunk, FINAL_ROW_CHUNK if final else ROW_CHUNK)


def _ffn(x2, mods, nw, wg, wu, wd, fnw, *, mod_base, seq, tm=1024, tf=512):
    m, d = x2.shape
    dff = wg.shape[1]
    tm = _tile(seq, tm)
    tf = _tile(dff, tf)
    tiles_per_batch = seq // tm
    final = fnw is not None
    in_specs = [
        pl.BlockSpec((tm, d), lambda i, f: (i, 0)),
        pl.BlockSpec((1, N_MOD, d), lambda i, f: (i // tiles_per_batch, 0, 0)),
        pl.BlockSpec((1, d), lambda i, f: (0, 0)),
        pl.BlockSpec((d, tf), lambda i, f: (0, f)),
        pl.BlockSpec((d, tf), lambda i, f: (0, f)),
        pl.BlockSpec((tf, d), lambda i, f: (f, 0)),
    ]
    args = [x2, mods, nw, wg, wu, wd]
    if final:
        in_specs.append(pl.BlockSpec((1, d), lambda i, f: (0, 0)))
        args.append(fnw)
    return pl.pallas_call(
        functools.partial(_ffn_kernel, mod_base=mod_base, final=final),
        grid=(m // tm, dff // tf),
        in_specs=in_specs,
        out_specs=pl.BlockSpec((tm, d), lambda i, f: (i, 0)),
        out_shape=jax.ShapeDtypeStruct((m, d), F32),
        scratch_shapes=[pltpu.VMEM((tm, d), BF16)],
        compiler_params=pltpu.CompilerParams(
            dimension_semantics=("parallel", "arbitrary"), vmem_limit_bytes=VMEM_LIMIT),
        name="ffn_final" if final else "ffn",
    )(*args)


def _inproj_kernel(x_ref, mods_ref, nw_ref, w_ref, ws_ref, wdw_ref, bdw_ref, lnw_ref, lnb_ref,
                   *rest, mod_base, tiles_per_batch, taps, rb, n_later):
    later_src = rest[:n_later]
    p_ref, ba_ref, y_ref = rest[n_later:n_later + 3]
    later_dst = rest[n_later + 3:2 * n_later + 3]
    h_scr, ca_s, buf, shf = rest[2 * n_later + 3:]
    i = pl.program_id(0)
    n = pl.program_id(1)
    tm = h_scr.shape[0]
    tq = tm // 4
    nshift = shf.shape[1]
    first = CONV_HALO - (taps - 1)

    def cast_later(k):
        if k < n_later:
            later_dst[k][...] = later_src[k][...].astype(later_dst[k].dtype)

    @pl.when(n == 0)
    def _():
        shift = mods_ref[0, mod_base:mod_base + 1, :]
        scale = mods_ref[0, mod_base + 1:mod_base + 2, :]
        _norm_mod_store(x_ref, h_scr, nw_ref[...], shift, scale)
        ba_ref[...] = jnp.dot(h_scr[...], ws_ref[...], preferred_element_type=F32)
        ca_s[...] = jnp.dot(h_scr[...], w_ref[...], preferred_element_type=F32)

    @pl.when(n == 1)
    def _():
        @pl.when(i % tiles_per_batch == 0)
        def _():
            buf[0:CONV_HALO, :] = jnp.zeros((CONV_HALO, buf.shape[1]), F32)

        cg = jnp.dot(h_scr[...], w_ref[...], preferred_element_type=F32)
        buf[CONV_HALO:CONV_HALO + tm, :] = ca_s[...] * _sigmoid(cg)
        cast_later(0)

    def conv_rows(r0, r):
        def window(off):
            a, b = divmod(off, HALO)
            if b == 0:
                return buf[r0 + HALO * a + r:r0 + HALO * a + r + rb, :]
            return shf[b - 1, HALO * a + r:HALO * a + r + rb, :]

        acc = wdw_ref[0:1, :] * window(first)
        for tap in range(1, taps):
            acc = acc + wdw_ref[tap:tap + 1, :] * window(first + tap)
        hh = acc + bdw_ref[...]
        mu = jnp.mean(hh, axis=-1, keepdims=True)
        cen = hh - mu
        var = jnp.mean(cen * cen, axis=-1, keepdims=True)
        hn = cen * lax.rsqrt(var + EPS) * lnw_ref[...] + lnb_ref[...]
        y_ref[r0 + r:r0 + r + rb, :] = _silu(hn).astype(y_ref.dtype)

    for qn in range(4):
        @pl.when(n == 2 + qn)
        def _(qn=qn):
            r0 = qn * tq
            for b in range(1, HALO):
                shf[b - 1] = buf[r0 + b:r0 + b + nshift, :]
            p_ref[...] = jnp.dot(h_scr[...], w_ref[...], preferred_element_type=F32)
            for r in range(0, tq, rb):
                conv_rows(r0, r)
            cast_later(qn + 1)
            if qn == 3:
                buf[0:CONV_HALO, :] = buf[tm:tm + CONV_HALO, :]


def _inproj(x2, mods, nw, w_main, w_small, w_dw, b_dw, ln_w, ln_b, later, *, mod_base, seq, tm=512,
            rb=32):
    m, d = x2.shape
    taps, cw = w_dw.shape
    assert taps - 1 <= CONV_HALO
    tm = _tile(seq, tm)
    tq = tm // 4
    rb = _tile(tq, rb)
    assert tq % HALO == 0
    tiles_per_batch = seq // tm
    n_tiles = m // tm
    bf16_sublanes = 2 * HALO
    for wl in later:
        assert wl.shape[0] % (n_tiles * bf16_sublanes) == 0, (wl.shape, n_tiles)
    assert len(later) <= 5

    def slab(k):
        rows = later[k].shape[0] // n_tiles
        return pl.BlockSpec((rows, later[k].shape[1]),
                            lambda i, n: (jnp.where(n >= 1 + k, i, jnp.maximum(i - 1, 0)), 0))
    vec = lambda: pl.BlockSpec((1, cw), lambda i, n: (0, 0))
    return pl.pallas_call(
        functools.partial(_inproj_kernel, mod_base=mod_base, tiles_per_batch=tiles_per_batch,
                          taps=taps, rb=rb, n_later=len(later)),
        grid=(n_tiles, 6),
        in_specs=[
            pl.BlockSpec((tm, d), lambda i, n: (i, 0)),
            pl.BlockSpec((1, N_MOD, d), lambda i, n: (i // tiles_per_batch, 0, 0)),
            pl.BlockSpec((1, d), lambda i, n: (0, 0)),
            pl.BlockSpec((d, cw), lambda i, n: (0, n)),
            pl.BlockSpec((d, LANES), lambda i, n: (0, 0)),
            pl.BlockSpec((taps, cw), lambda i, n: (0, 0)),
            vec(), vec(), vec(),
        ] + [slab(k) for k in range(len(later))],
        out_specs=[
            pl.BlockSpec((tm, cw), lambda i, n: (i, jnp.maximum(n - 2, 0))),
            pl.BlockSpec((tm, LANES), lambda i, n: (i, 0)),
            pl.BlockSpec((tm, cw), lambda i, n: (i, 0)),
        ] + [slab(k) for k in range(len(later))],
        out_shape=[
            jax.ShapeDtypeStruct((m, 4 * cw), F32),
            jax.ShapeDtypeStruct((m, LANES), F32),
            jax.ShapeDtypeStruct((m, cw), BF16),
        ] + [jax.ShapeDtypeStruct(wl.shape, BF16) for wl in later],
        scratch_shapes=[pltpu.VMEM((tm, d), BF16), pltpu.VMEM((tm, cw), F32),
                        pltpu.VMEM((tm + CONV_HALO, cw), F32),
                        pltpu.VMEM((HALO - 1, tq + CONV_HALO - HALO, cw), F32)],
        compiler_params=pltpu.CompilerParams(
            dimension_semantics=("arbitrary", "arbitrary"), vmem_limit_bytes=VMEM_LIMIT),
        name="in_proj",
    )(x2, mods, nw, w_main, w_small, w_dw, b_dw, ln_w, ln_b, *later)


def _dn_kernel(q0_ref, k0_ref, v0_ref, ba0_ref, qn_ref, kn_ref, vn_ref, ban_ref, z_ref, wsh_ref,
               hp_ref, onw_ref, y_ref, halo_s, qkv_s, gate_s, gt_s, state, *, heads, dh, taps,
               npairs):
    t, c = DN_TILE, DN_CHUNK
    ts = npairs * t
    ng = 2 * heads
    w = heads * dh
    j = pl.program_id(1)

    row = lax.broadcasted_iota(jnp.int32, (t, t), 0)
    col = lax.broadcasted_iota(jnp.int32, (t, t), 1)
    same = (row // c) == (col // c)
    tril = same & (row >= col)
    strict = same & (row > col)
    triu = same & (row <= col)
    eye = (row == col).astype(F32)
    blk = lambda n: (row // n) == (col // n)
    base_mask = blk(INV_BASE)
    merge_masks = []
    n = INV_BASE
    while n < c:
        merge_masks.append(blk(2 * n) & jnp.logical_not(blk(n)))
        n *= 2

    def make_tasks(q_src, k_src, v_src, ba_src):
        def gates_task(p):
            def run():
                ba_t = ba_src[0, p * t:(p + 1) * t, :].T[0:ng, :]
                sig_t = _sigmoid(ba_t)
                xx = ba_t + hp_ref[:, 1:2]
                softplus = jnp.maximum(xx, 0.0) + jnp.log1p(jnp.exp(-jnp.abs(xx)))
                gg_t = -jnp.exp(hp_ref[:, 0:1]) * softplus
                g_t = jnp.dot(gg_t, triu.astype(F32), preferred_element_type=F32,
                              precision=lax.Precision.HIGHEST)
                rid = lax.broadcasted_iota(jnp.int32, (ng, t), 0)
                packed = jnp.where(rid < heads, sig_t, g_t)
                gate_s[p * t:(p + 1) * t, :] = jnp.concatenate(
                    [packed, jnp.zeros((LANES - ng, t), F32)], axis=0).T
                gt_s[p] = g_t
            return run

        def conv_task(src, base, h, gain):
            lo = base + h * dh

            def run():
                xs = jnp.concatenate([halo_s[:, lo:lo + dh], src[0, :, h * dh:(h + 1) * dh]],
                                     axis=0)
                acc = wsh_ref[0:1, lo:lo + dh] * xs
                for tap in range(1, taps):
                    acc = pltpu.roll(acc, 1, 0) + wsh_ref[tap:tap + 1, lo:lo + dh] * xs
                y = _silu(acc[HALO:, :])
                if gain is not None:
                    y = y * (lax.rsqrt(jnp.sum(y * y, axis=-1, keepdims=True) + EPS) * gain)
                qkv_s[:, lo:lo + dh] = y
            return run

        def halo_task():
            halo_s[:, 0:w] = q_src[0, ts - HALO:ts, :]
            halo_s[:, w:2 * w] = k_src[0, ts - HALO:ts, :]
            halo_s[:, 2 * w:3 * w] = v_src[0, ts - HALO:ts, :]

        tasks = [gates_task(p) for p in range(npairs)]
        for h in range(heads):
            tasks += [conv_task(q_src, 0, h, dh ** -0.5), conv_task(k_src, w, h, 1.0),
                      conv_task(v_src, 2 * w, h, None)]
        tasks.append(halo_task)
        return tasks

    @pl.when(j == 0)
    def _():
        halo_s[...] = jnp.zeros_like(halo_s)
        state[...] = jnp.zeros_like(state)
        for task in make_tasks(q0_ref, k0_ref, v0_ref, ba0_ref):
            task()

    units = [(p, h) for p in range(npairs) for h in range(heads)]
    us = range(len(units))
    prow = lambda p: slice(p * t, (p + 1) * t)
    q = [qkv_s[prow(p), h * dh:(h + 1) * dh] for p, h in units]
    k = [qkv_s[prow(p), w + h * dh:w + (h + 1) * dh] for p, h in units]
    v = [qkv_s[prow(p), 2 * w + h * dh:2 * w + (h + 1) * dh] for p, h in units]
    gate_cols = [gate_s[prow(p), :] for p in range(npairs)]
    gcum_t = [gt_s[p] for p in range(npairs)]

    tasks = make_tasks(qn_ref, kn_ref, vn_ref, ban_ref)
    n_stages = 11 + 2 * npairs * (t // c)
    per_stage = -(-len(tasks) // n_stages)

    def prep():
        for _ in range(per_stage):
            if tasks:
                tasks.pop(0)()

    beta = [gate_cols[p][:, h:h + 1] for p, h in units]
    g_col = [gate_cols[p][:, heads + h:heads + h + 1] for p, h in units]
    g_row = [gcum_t[p][heads + h:heads + h + 1, :] for p, h in units]
    e_g = [jnp.exp(g) for g in g_col]
    decay = [jnp.exp(jnp.where(tril, g_col[u] - g_row[u], -jnp.inf)) for u in us]
    kb = [k[u] * beta[u] for u in us]
    aq = [lax.dot_general(jnp.concatenate([kb[u], q[u]], axis=0).astype(BF16), k[u].astype(BF16),
                          (((1,), (1,)), ((), ())), preferred_element_type=F32) for u in us]
    prep()
    a_intra = [aq[u][t:] * decay[u] for u in us]
    lmat = [aq[u][:t] * jnp.where(strict, decay[u], 0.0) for u in us]

    xp = [-jnp.where(base_mask, lmat[u], 0.0) for u in us]
    rr = [eye + xp[u] for u in us]
    xp = [_bdot(xp[u], xp[u]) for u in us]
    prep()
    for _ in range(INV_BASE.bit_length() - 3):
        pr = [_bdot(jnp.concatenate([rr[u], xp[u]], axis=0), xp[u]) for u in us]
        prep()
        rr = [rr[u] + pr[u][:t] for u in us]
        xp = [pr[u][t:] for u in us]
    pr = [_bdot(rr[u], xp[u]) for u in us]
    prep()
    rr = [rr[u] + pr[u] for u in us]
    for off_mask in merge_masks:
        pr = [_bdot(rr[u], jnp.where(off_mask, lmat[u], 0.0)) for u in us]
        prep()
        pr = [_bdot(pr[u], rr[u]) for u in us]
        prep()
        rr = [rr[u] - pr[u] for u in us]

    sol = [_bdot(rr[u], jnp.concatenate([v[u] * beta[u], kb[u] * e_g[u]], axis=1)) for u in us]
    prep()
    qg = [q[u] * e_g[u] for u in us]

    hs = range(heads)
    st = [state[h] for h in hs]
    zeros = jnp.zeros((c, dh), F32)
    for p in range(npairs):
        un = [p * heads + h for h in hs]
        for ci in range(t // c):
            rows = slice(ci * c, (ci + 1) * c)
            out_rows = slice(p * t + ci * c, p * t + (ci + 1) * c)
            wq = [_bdot(jnp.concatenate([sol[un[h]][rows, dh:], qg[un[h]][rows]], axis=0), st[h])
                  for h in hs]
            prep()
            v_new = [sol[un[h]][rows, :dh] - wq[h][:c] for h in hs]
            g_last = [g_col[un[h]][ci * c + c - 1:ci * c + c, :] for h in hs]
            kd = [k[un[h]][rows] * jnp.exp(g_last[h] - g_col[un[h]][rows]) for h in hs]
            upd = [lax.dot_general(kd[h].astype(BF16), v_new[h].astype(BF16),
                                   (((0,), (0,)), ((), ())), preferred_element_type=F32) for h in hs]
            prep()
            st = [st[h] * jnp.exp(g_last[h]) + upd[h] for h in hs]
            for h in hs:
                parts = [zeros] * (t // c)
                parts[ci] = v_new[h]
                v_pad = jnp.concatenate(parts, axis=0)
                o = wq[h][c:] + _bdot(a_intra[un[h]][rows], v_pad)
                on = o * lax.rsqrt(jnp.mean(o * o, axis=-1, keepdims=True) + EPS) * onw_ref[...]
                zz = z_ref[0, out_rows, h * dh:(h + 1) * dh]
                y_ref[0, out_rows, h * dh:(h + 1) * dh] = (on * _silu(zz)).astype(y_ref.dtype)
    while tasks:
        tasks.pop(0)()
    for h in hs:
        state[h] = st[h]


def _deltanet(p3, ba3, w_short, hp, onw, *, heads, dh, col0):
    b, s, _ = p3.shape
    w = heads * dh
    taps = w_short.shape[0]
    npairs = DN_PAIRS if s % (DN_PAIRS * DN_TILE) == 0 else 1
    ts = npairs * DN_TILE
    assert s % ts == 0 and col0 % w == 0 and dh == LANES and DN_TILE == LANES
    assert taps - 1 <= HALO
    cb = col0 // w
    nt = s // ts
    nxt = lambda j: jnp.minimum(j + 1, nt - 1)
    tile0 = lambda cc: pl.BlockSpec((1, ts, w), lambda i, j: (i, 0, cc))
    tilen = lambda cc: pl.BlockSpec((1, ts, w), lambda i, j: (i, nxt(j), cc))
    return pl.pallas_call(
        functools.partial(_dn_kernel, heads=heads, dh=dh, taps=taps, npairs=npairs),
        grid=(b, nt),
        in_specs=[
            tile0(cb), tile0(cb + 1), tile0(cb + 2),
            pl.BlockSpec((1, ts, LANES), lambda i, j: (i, 0, 0)),
            tilen(cb), tilen(cb + 1), tilen(cb + 2),
            pl.BlockSpec((1, ts, LANES), lambda i, j: (i, nxt(j), 0)),
            pl.BlockSpec((1, ts, w), lambda i, j: (i, j, cb + 3)),
            pl.BlockSpec((taps, 3 * w), lambda i, j: (0, 0)),
            pl.BlockSpec((2 * heads, 2), lambda i, j: (0, 0)),
            pl.BlockSpec((1, dh), lambda i, j: (0, 0)),
        ],
        out_specs=pl.BlockSpec((1, ts, w), lambda i, j: (i, j, 0)),
        out_shape=jax.ShapeDtypeStruct((b, s, w), BF16),
        scratch_shapes=[pltpu.VMEM((HALO, 3 * w), F32), pltpu.VMEM((ts, 3 * w), F32),
                        pltpu.VMEM((ts, LANES), F32), pltpu.VMEM((npairs, 2 * heads, DN_TILE), F32),
                        pltpu.VMEM((heads, dh, dh), F32)],
        compiler_params=pltpu.CompilerParams(
            dimension_semantics=("parallel", "arbitrary"), vmem_limit_bytes=VMEM_LIMIT),
        name="deltanet",
    )(p3, p3, p3, ba3, p3, p3, p3, ba3, p3, w_short, hp, onw)


def _outproj_kernel(x_ref, mods_ref, yc_ref, yd_ref, wc_ref, wd_ref, o_ref, *, mod_idx):
    y = jnp.dot(yc_ref[...], wc_ref[...], preferred_element_type=F32)
    y = y + jnp.dot(yd_ref[...], wd_ref[...], preferred_element_type=F32)
    o_ref[...] = x_ref[...] + mods_ref[0, mod_idx:mod_idx + 1, :] * y


def _outproj(x2, mods, yc, yd, w_o, *, mod_idx, seq, tm=512):
    m, d = x2.shape
    half = w_o.shape[0] // 2
    assert yc.shape[1] == half and yd.shape[1] == half
    tm = _tile(seq, tm)
    tiles_per_batch = seq // tm
    return pl.pallas_call(
        functools.partial(_outproj_kernel, mod_idx=mod_idx),
        grid=(m // tm,),
        in_specs=[
            pl.BlockSpec((tm, d), lambda i: (i, 0)),
            pl.BlockSpec((1, N_MOD, d), lambda i: (i // tiles_per_batch, 0, 0)),
            pl.BlockSpec((tm, yc.shape[1]), lambda i: (i, 0)),
            pl.BlockSpec((tm, yd.shape[1]), lambda i: (i, 0)),
            pl.BlockSpec((half, d), lambda i: (0, 0)),
            pl.BlockSpec((half, d), lambda i: (1, 0)),
        ],
        out_specs=pl.BlockSpec((tm, d), lambda i: (i, 0)),
        out_shape=jax.ShapeDtypeStruct((m, d), F32),
        compiler_params=pltpu.CompilerParams(
            dimension_semantics=("parallel",), vmem_limit_bytes=VMEM_LIMIT),
        name="out_proj",
    )(x2, mods, yc, yd, w_o, w_o)


def kernel(x, c, w_ada, b_ada, ffn1_norm, ffn1_wg, ffn1_wu, ffn1_wd, mix_norm, w_in, w_dw, b_dw,
           conv_ln_w, conv_ln_b, w_short, a_log, dt_bias, dn_norm_w, w_out, ffn2_norm, ffn2_wg,
           ffn2_wu, ffn2_wd, final_norm):
    b, s, d = x.shape
    depth = w_ada.shape[0]
    heads = a_log.shape[1]
    dh = dn_norm_w.shape[1]
    cw = w_dw.shape[2]
    dnw = heads * dh
    n_main = 2 * cw + 4 * dnw
    assert w_in.shape[2] == n_main + 2 * heads and 2 * heads <= LANES and cw == dnw and depth >= 1

    x2 = x.reshape(b * s, d)
    c_pad = jnp.pad(c, ((0, (-b) % HALO), (0, 0)))
    row = lambda v: v.reshape(1, -1)
    for l in range(depth):
        mods = _mods(c_pad, w_ada[l], row(b_ada[l]))[:b].reshape(b, N_MOD, d)
        last = l == depth - 1

        x2 = _ffn(x2, mods, row(ffn1_norm[l]), ffn1_wg[l].astype(BF16), ffn1_wu[l].astype(BF16),
                  ffn1_wd[l].astype(BF16), None, mod_base=0, seq=s)

        w_main = w_in[l].astype(BF16)
        w_small = jnp.pad(w_in[l][:, n_main:], ((0, 0), (0, LANES - 2 * heads))).astype(BF16)
        later = [ffn2_wg[l], ffn2_wu[l], ffn2_wd[l], w_out[l]]
        p, ba, y_conv, wg2, wu2, wd2, wo = _inproj(
            x2, mods, row(mix_norm[l]), w_main, w_small, w_dw[l], row(b_dw[l]), row(conv_ln_w[l]),
            row(conv_ln_b[l]), later, mod_base=3, seq=s)
        p3 = p.reshape(b, s, 4 * dnw)
        ba3 = ba.reshape(b, s, LANES)

        zh = jnp.zeros((heads,), F32)
        hp = jnp.stack([jnp.concatenate([zh, a_log[l]]), jnp.concatenate([zh, dt_bias[l]])], axis=1)
        y_dn = _deltanet(p3, ba3, w_short[l], hp, row(dn_norm_w[l]), heads=heads, dh=dh, col0=0)

        x2 = _outproj(x2, mods, y_conv, y_dn.reshape(b * s, dnw),
                      wo, mod_idx=5, seq=s)

        x2 = _ffn(x2, mods, row(ffn2_norm[l]), wg2, wu2, wd2,
                  row(final_norm) if last else None, mod_base=6, seq=s)
    return x2.reshape(b, s, d)
```

```python
import functools

import jax
import jax.numpy as jnp
from jax import lax
from jax.experimental import pallas as pl
from jax.experimental.pallas import tpu as pltpu

F32 = jnp.float32
BF16 = jnp.bfloat16

EPS = 1e-6
N_MOD = 9
DN_CHUNK = 64
DN_TILE = 2 * DN_CHUNK
DN_PAIRS = 2
INV_BASE = 8
LANES = 128
HALO = 8
ROW_CHUNK = 32
ROW_UNROLL = 4
FINAL_ROW_CHUNK = 64
CONV_HALO = 32
VMEM_LIMIT = 60 * 1024 * 1024


def _tile(n, pref):
    t = min(n, pref)
    assert n % t == 0, (n, pref)
    return t


def _sigmoid(v):
    return jax.nn.sigmoid(v)


def _silu(v):
    return v * _sigmoid(v)


def _bdot(a, b):
    return jnp.dot(a.astype(BF16), b.astype(BF16), preferred_element_type=F32)


def _row_chunks(n_rows, fn, rows_per_chunk=ROW_CHUNK):
    rc = min(rows_per_chunk, n_rows)
    assert n_rows % rc == 0

    def body(i, carry):
        fn(pl.ds(pl.multiple_of(i * rc, rc), rc))
        return carry

    trips = n_rows // rc
    lax.fori_loop(0, trips, body, 0, unroll=min(ROW_UNROLL, trips))


def _norm_mod_store(x_ref, h_ref, nw, shift, scale, zero_ref=None):
    gain = nw * (1.0 + scale)

    def chunk(rows):
        x = x_ref[rows, :]
        ms = jnp.mean(x * x, axis=-1, keepdims=True)
        h_ref[rows, :] = (x * lax.rsqrt(ms + EPS) * gain + shift).astype(h_ref.dtype)
        if zero_ref is not None:
            zero_ref[rows, :] = jnp.zeros_like(x)

    _row_chunks(x_ref.shape[0], chunk)


def _mods_kernel(c_ref, w_ref, b_ref, o_ref):
    s = _silu(c_ref[...])
    o_ref[...] = _bdot(s, w_ref[...]) + b_ref[...]


def _mods(c_pad, w_ada, b_ada):
    rows, d = c_pad.shape
    n = w_ada.shape[1]
    tn = _tile(d, 1024)
    assert n % tn == 0
    return pl.pallas_call(
        _mods_kernel,
        grid=(n // tn,),
        in_specs=[
            pl.BlockSpec((rows, d), lambda j: (0, 0)),
            pl.BlockSpec((d, tn), lambda j: (0, j)),
            pl.BlockSpec((1, tn), lambda j: (0, j)),
        ],
        out_specs=pl.BlockSpec((rows, tn), lambda j: (0, j)),
        out_shape=jax.ShapeDtypeStruct((rows, n), F32),
        compiler_params=pltpu.CompilerParams(
            dimension_semantics=("parallel",), vmem_limit_bytes=VMEM_LIMIT),
        name="mods",
    )(c_pad, w_ada, b_ada)


def _ffn_kernel(x_ref, mods_ref, nw_ref, wg_ref, wu_ref, wd_ref, *rest, mod_base, final):
    if final:
        fnw_ref, o_ref, h_scr = rest
    else:
        o_ref, h_scr = rest
    f = pl.program_id(1)

    @pl.when(f == 0)
    def _():
        shift = mods_ref[0, mod_base:mod_base + 1, :]
        scale = mods_ref[0, mod_base + 1:mod_base + 2, :]
        _norm_mod_store(x_ref, h_scr, nw_ref[...], shift, scale, zero_ref=o_ref)

    h = h_scr[...]
    g = jnp.dot(h, wg_ref[...], preferred_element_type=F32)
    u = jnp.dot(h, wu_ref[...], preferred_element_type=F32)
    a = (_silu(g) * u).astype(BF16)
    o_ref[...] += jnp.dot(a, wd_ref[...], preferred_element_type=F32)

    @pl.when(f == pl.num_programs(1) - 1)
    def _():
        gate = mods_ref[0, mod_base + 2:mod_base + 3, :]

        def chunk(rows):
            y = x_ref[rows, :] + 0.5 * gate * o_ref[rows, :]
            if final:
                ms = jnp.mean(y * y, axis=-1, keepdims=True)
                y = y * lax.rsqrt(ms + EPS) * fnw_ref[...]
            o_ref[rows, :] = y

        _row_chunks(o_ref.shape[0], chunk, FINAL_ROW_CHUNK if final else ROW_CHUNK)


def _ffn(x2, mods, nw, wg, wu, wd, fnw, *, mod_base, seq, tm=1024, tf=512):
    m, d = x2.shape
    dff = wg.shape[1]
    tm = _tile(seq, tm)
    tf = _tile(dff, tf)
    tiles_per_batch = seq // tm
    final = fnw is not None
    in_specs = [
        pl.BlockSpec((tm, d), lambda i, f: (i, 0)),
        pl.BlockSpec((1, N_MOD, d), lambda i, f: (i // tiles_per_batch, 0, 0)),
        pl.BlockSpec((1, d), lambda i, f: (0, 0)),
        pl.BlockSpec((d, tf), lambda i, f: (0, f)),
        pl.BlockSpec((d, tf), lambda i, f: (0, f)),
        pl.BlockSpec((tf, d), lambda i, f: (f, 0)),
    ]
    args = [x2, mods, nw, wg, wu, wd]
    if final:
        in_specs.append(pl.BlockSpec((1, d), lambda i, f: (0, 0)))
        args.append(fnw)
    return pl.pallas_call(
        functools.partial(_ffn_kernel, mod_base=mod_base, final=final),
        grid=(m // tm, dff // tf),
        in_specs=in_specs,
        out_specs=pl.BlockSpec((tm, d), lambda i, f: (i, 0)),
        out_shape=jax.ShapeDtypeStruct((m, d), F32),
        scratch_shapes=[pltpu.VMEM((tm, d), BF16)],
        compiler_params=pltpu.CompilerParams(
            dimension_semantics=("parallel", "arbitrary"), vmem_limit_bytes=VMEM_LIMIT),
        name="ffn_final" if final else "ffn",
    )(*args)


def _inproj_kernel(x_ref, mods_ref, nw_ref, w_ref, ws_ref, wdw_ref, bdw_ref, lnw_ref, lnb_ref,
                   *rest, mod_base, tiles_per_batch, taps, rb, n_later):
    later_src = rest[:n_later]
    p_ref, ba_ref, y_ref = rest[n_later:n_later + 3]
    later_dst = rest[n_later + 3:2 * n_later + 3]
    h_scr, ca_s, buf, shf = rest[2 * n_later + 3:]
    i = pl.program_id(0)
    n = pl.program_id(1)
    tm = h_scr.shape[0]
    tq = tm // 4
    nshift = shf.shape[1]
    first = CONV_HALO - (taps - 1)

    def cast_later(k):
        if k < n_later:
            later_dst[k][...] = later_src[k][...].astype(later_dst[k].dtype)

    @pl.when(n == 0)
    def _():
        shift = mods_ref[0, mod_base:mod_base + 1, :]
        scale = mods_ref[0, mod_base + 1:mod_base + 2, :]
        _norm_mod_store(x_ref, h_scr, nw_ref[...], shift, scale)
        ba_ref[...] = jnp.dot(h_scr[...], ws_ref[...], preferred_element_type=F32)
        ca_s[...] = jnp.dot(h_scr[...], w_ref[...], preferred_element_type=F32)

    @pl.when(n == 1)
    def _():
        @pl.when(i % tiles_per_batch == 0)
        def _():
            buf[0:CONV_HALO, :] = jnp.zeros((CONV_HALO, buf.shape[1]), F32)

        cg = jnp.dot(h_scr[...], w_ref[...], preferred_element_type=F32)
        buf[CONV_HALO:CONV_HALO + tm, :] = ca_s[...] * _sigmoid(cg)
        cast_later(0)

    def conv_rows(r0, r):
        def window(off):
            a, b = divmod(off, HALO)
            if b == 0:
                return buf[r0 + HALO * a + r:r0 + HALO * a + r + rb, :]
            return shf[b - 1, HALO * a + r:HALO * a + r + rb, :]

        acc = wdw_ref[0:1, :] * window(first)
        for tap in range(1, taps):
            acc = acc + wdw_ref[tap:tap + 1, :] * window(first + tap)
        hh = acc + bdw_ref[...]
        mu = jnp.mean(hh, axis=-1, keepdims=True)
        cen = hh - mu
        var = jnp.mean(cen * cen, axis=-1, keepdims=True)
        hn = cen * lax.rsqrt(var + EPS) * lnw_ref[...] + lnb_ref[...]
        y_ref[r0 + r:r0 + r + rb, :] = _silu(hn).astype(y_ref.dtype)

    for qn in range(4):
        @pl.when(n == 2 + qn)
        def _(qn=qn):
            r0 = qn * tq
            for b in range(1, HALO):
                shf[b - 1] = buf[r0 + b:r0 + b + nshift, :]
            p_ref[...] = jnp.dot(h_scr[...], w_ref[...], preferred_element_type=F32)
            for r in range(0, tq, rb):
                conv_rows(r0, r)
            cast_later(qn + 1)
            if qn == 3:
                buf[0:CONV_HALO, :] = buf[tm:tm + CONV_HALO, :]


def _inproj(x2, mods, nw, w_main, w_small, w_dw, b_dw, ln_w, ln_b, later, *, mod_base, seq, tm=512,
            rb=32):
    m, d = x2.shape
    taps, cw = w_dw.shape
    assert taps - 1 <= CONV_HALO
    tm = _tile(seq, tm)
    tq = tm // 4
    rb = _tile(tq, rb)
    assert tq % HALO == 0
    tiles_per_batch = seq // tm
    n_tiles = m // tm
    bf16_sublanes = 2 * HALO
    for wl in later:
        assert wl.shape[0] % (n_tiles * bf16_sublanes) == 0, (wl.shape, n_tiles)
    assert len(later) <= 5

    def slab(k):
        rows = later[k].shape[0] // n_tiles
        return pl.BlockSpec((rows, later[k].shape[1]),
                            lambda i, n: (jnp.where(n >= 1 + k, i, jnp.maximum(i - 1, 0)), 0))
    vec = lambda: pl.BlockSpec((1, cw), lambda i, n: (0, 0))
    return pl.pallas_call(
        functools.partial(_inproj_kernel, mod_base=mod_base, tiles_per_batch=tiles_per_batch,
                          taps=taps, rb=rb, n_later=len(later)),
        grid=(n_tiles, 6),
        in_specs=[
            pl.BlockSpec((tm, d), lambda i, n: (i, 0)),
            pl.BlockSpec((1, N_MOD, d), lambda i, n: (i // tiles_per_batch, 0, 0)),
            pl.BlockSpec((1, d), lambda i, n: (0, 0)),
            pl.BlockSpec((d, cw), lambda i, n: (0, n)),
            pl.BlockSpec((d, LANES), lambda i, n: (0, 0)),
            pl.BlockSpec((taps, cw), lambda i, n: (0, 0)),
            vec(), vec(), vec(),
        ] + [slab(k) for k in range(len(later))],
        out_specs=[
            pl.BlockSpec((tm, cw), lambda i, n: (i, jnp.maximum(n - 2, 0))),
            pl.BlockSpec((tm, LANES), lambda i, n: (i, 0)),
            pl.BlockSpec((tm, cw), lambda i, n: (i, 0)),
        ] + [slab(k) for k in range(len(later))],
        out_shape=[
            jax.ShapeDtypeStruct((m, 4 * cw), F32),
            jax.ShapeDtypeStruct((m, LANES), F32),
            jax.ShapeDtypeStruct((m, cw), BF16),
        ] + [jax.ShapeDtypeStruct(wl.shape, BF16) for wl in later],
        scratch_shapes=[pltpu.VMEM((tm, d), BF16), pltpu.VMEM((tm, cw), F32),
                        pltpu.VMEM((tm + CONV_HALO, cw), F32),
                        pltpu.VMEM((HALO - 1, tq + CONV_HALO - HALO, cw), F32)],
        compiler_params=pltpu.CompilerParams(
            dimension_semantics=("arbitrary", "arbitrary"), vmem_limit_bytes=VMEM_LIMIT,
            allow_input_fusion=[k == 3 for k in range(9 + len(later))]),
        name="in_proj",
    )(x2, mods, nw, w_main, w_small, w_dw, b_dw, ln_w, ln_b, *later)


def _dn_kernel(q0_ref, k0_ref, v0_ref, ba0_ref, qn_ref, kn_ref, vn_ref, ban_ref, z_ref, wsh_ref,
               hp_ref, onw_ref, y_ref, halo_s, qkv_s, gate_s, gt_s, state, *, heads, dh, taps,
               npairs):
    t, c = DN_TILE, DN_CHUNK
    ts = npairs * t
    ng = 2 * heads
    w = heads * dh
    j = pl.program_id(1)

    row = lax.broadcasted_iota(jnp.int32, (t, t), 0)
    col = lax.broadcasted_iota(jnp.int32, (t, t), 1)
    same = (row // c) == (col // c)
    tril = same & (row >= col)
    strict = same & (row > col)
    triu = same & (row <= col)
    eye = (row == col).astype(F32)
    blk = lambda n: (row // n) == (col // n)
    base_mask = blk(INV_BASE)
    merge_masks = []
    n = INV_BASE
    while n < c:
        merge_masks.append(blk(2 * n) & jnp.logical_not(blk(n)))
        n *= 2

    def make_tasks(q_src, k_src, v_src, ba_src):
        def gates_task(p):
            def run():
                ba_t = ba_src[0, p * t:(p + 1) * t, :].T[0:ng, :]
                sig_t = _sigmoid(ba_t)
                xx = ba_t + hp_ref[:, 1:2]
                softplus = jnp.maximum(xx, 0.0) + jnp.log1p(jnp.exp(-jnp.abs(xx)))
                gg_t = -jnp.exp(hp_ref[:, 0:1]) * softplus
                g_t = jnp.dot(gg_t, triu.astype(F32), preferred_element_type=F32,
                              precision=lax.Precision.HIGHEST)
                rid = lax.broadcasted_iota(jnp.int32, (ng, t), 0)
                packed = jnp.where(rid < heads, sig_t, g_t)
                gate_s[p * t:(p + 1) * t, :] = jnp.concatenate(
                    [packed, jnp.zeros((LANES - ng, t), F32)], axis=0).T
                gt_s[p] = g_t
            return run

        def conv_task(src, base, h, gain):
            lo = base + h * dh

            def run():
                xs = jnp.concatenate([halo_s[:, lo:lo + dh], src[0, :, h * dh:(h + 1) * dh]],
                                     axis=0)
                acc = wsh_ref[0:1, lo:lo + dh] * xs
                for tap in range(1, taps):
                    acc = pltpu.roll(acc, 1, 0) + wsh_ref[tap:tap + 1, lo:lo + dh] * xs
                y = _silu(acc[HALO:, :])
                if gain is not None:
                    y = y * (lax.rsqrt(jnp.sum(y * y, axis=-1, keepdims=True) + EPS) * gain)
                qkv_s[:, lo:lo + dh] = y
            return run

        def halo_task():
            halo_s[:, 0:w] = q_src[0, ts - HALO:ts, :]
            halo_s[:, w:2 * w] = k_src[0, ts - HALO:ts, :]
            halo_s[:, 2 * w:3 * w] = v_src[0, ts - HALO:ts, :]

        tasks = [gates_task(p) for p in range(npairs)]
        for h in range(heads):
            tasks += [conv_task(q_src, 0, h, dh ** -0.5), conv_task(k_src, w, h, 1.0),
                      conv_task(v_src, 2 * w, h, None)]
        tasks.append(halo_task)
        return tasks

    @pl.when(j == 0)
    def _():
        halo_s[...] = jnp.zeros_like(halo_s)
        state[...] = jnp.zeros_like(state)
        for task in make_tasks(q0_ref, k0_ref, v0_ref, ba0_ref):
            task()

    units = [(p, h) for p in range(npairs) for h in range(heads)]
    us = range(len(units))
    prow = lambda p: slice(p * t, (p + 1) * t)
    q = [qkv_s[prow(p), h * dh:(h + 1) * dh] for p, h in units]
    k = [qkv_s[prow(p), w + h * dh:w + (h + 1) * dh] for p, h in units]
    v = [qkv_s[prow(p), 2 * w + h * dh:2 * w + (h + 1) * dh] for p, h in units]
    gate_cols = [gate_s[prow(p), :] for p in range(npairs)]
    gcum_t = [gt_s[p] for p in range(npairs)]

    tasks = make_tasks(qn_ref, kn_ref, vn_ref, ban_ref)
    n_stages = 11 + 2 * npairs * (t // c)
    per_stage = -(-len(tasks) // n_stages)

    def prep():
        for _ in range(per_stage):
            if tasks:
                tasks.pop(0)()

    beta = [gate_cols[p][:, h:h + 1] for p, h in units]
    g_col = [gate_cols[p][:, heads + h:heads + h + 1] for p, h in units]
    g_row = [gcum_t[p][heads + h:heads + h + 1, :] for p, h in units]
    e_g = [jnp.exp(g) for g in g_col]
    decay = [jnp.exp(jnp.where(tril, g_col[u] - g_row[u], -jnp.inf)) for u in us]
    kb = [k[u] * beta[u] for u in us]
    aq = [lax.dot_general(jnp.concatenate([kb[u], q[u]], axis=0).astype(BF16), k[u].astype(BF16),
                          (((1,), (1,)), ((), ())), preferred_element_type=F32) for u in us]
    prep()
    a_intra = [aq[u][t:] * decay[u] for u in us]
    lmat = [aq[u][:t] * jnp.where(strict, decay[u], 0.0) for u in us]

    xp = [-jnp.where(base_mask, lmat[u], 0.0) for u in us]
    rr = [eye + xp[u] for u in us]
    xp = [_bdot(xp[u], xp[u]) for u in us]
    prep()
    for _ in range(INV_BASE.bit_length() - 3):
        pr = [_bdot(jnp.concatenate([rr[u], xp[u]], axis=0), xp[u]) for u in us]
        prep()
        rr = [rr[u] + pr[u][:t] for u in us]
        xp = [pr[u][t:] for u in us]
    pr = [_bdot(rr[u], xp[u]) for u in us]
    prep()
    rr = [rr[u] + pr[u] for u in us]
    for off_mask in merge_masks:
        pr = [_bdot(rr[u], jnp.where(off_mask, lmat[u], 0.0)) for u in us]
        prep()
        pr = [_bdot(pr[u], rr[u]) for u in us]
        prep()
        rr = [rr[u] - pr[u] for u in us]

    sol = [_bdot(rr[u], jnp.concatenate([v[u] * beta[u], kb[u] * e_g[u]], axis=1)) for u in us]
    prep()
    qg = [q[u] * e_g[u] for u in us]

    hs = range(heads)
    st = [state[h] for h in hs]
    zeros = jnp.zeros((c, dh), F32)
    for p in range(npairs):
        un = [p * heads + h for h in hs]
        for ci in range(t // c):
            rows = slice(ci * c, (ci + 1) * c)
            out_rows = slice(p * t + ci * c, p * t + (ci + 1) * c)
            wq = [_bdot(jnp.concatenate([sol[un[h]][rows, dh:], qg[un[h]][rows]], axis=0), st[h])
                  for h in hs]
            prep()
            v_new = [sol[un[h]][rows, :dh] - wq[h][:c] for h in hs]
            g_last = [g_col[un[h]][ci * c + c - 1:ci * c + c, :] for h in hs]
            kd = [k[un[h]][rows] * jnp.exp(g_last[h] - g_col[un[h]][rows]) for h in hs]
            upd = [lax.dot_general(kd[h].astype(BF16), v_new[h].astype(BF16),
                                   (((0,), (0,)), ((), ())), preferred_element_type=F32) for h in hs]
            prep()
            st = [st[h] * jnp.exp(g_last[h]) + upd[h] for h in hs]
            for h in hs:
                parts = [zeros] * (t // c)
                parts[ci] = v_new[h]
                v_pad = jnp.concatenate(parts, axis=0)
                o = wq[h][c:] + _bdot(a_intra[un[h]][rows], v_pad)
                on = o * lax.rsqrt(jnp.mean(o * o, axis=-1, keepdims=True) + EPS) * onw_ref[...]
                zz = z_ref[0, out_rows, h * dh:(h + 1) * dh]
                y_ref[0, out_rows, h * dh:(h + 1) * dh] = (on * _silu(zz)).astype(y_ref.dtype)
    while tasks:
        tasks.pop(0)()
    for h in hs:
        state[h] = st[h]


def _deltanet(p3, ba3, w_short, hp, onw, *, heads, dh, col0):
    b, s, _ = p3.shape
    w = heads * dh
    taps = w_short.shape[0]
    npairs = DN_PAIRS if s % (DN_PAIRS * DN_TILE) == 0 else 1
    ts = npairs * DN_TILE
    assert s % ts == 0 and col0 % w == 0 and dh == LANES and DN_TILE == LANES
    assert taps - 1 <= HALO
    cb = col0 // w
    nt = s // ts
    nxt = lambda j: jnp.minimum(j + 1, nt - 1)
    tile0 = lambda cc: pl.BlockSpec((1, ts, w), lambda i, j: (i, 0, cc))
    tilen = lambda cc: pl.BlockSpec((1, ts, w), lambda i, j: (i, nxt(j), cc))
    return pl.pallas_call(
        functools.partial(_dn_kernel, heads=heads, dh=dh, taps=taps, npairs=npairs),
        grid=(b, nt),
        in_specs=[
            tile0(cb), tile0(cb + 1), tile0(cb + 2),
            pl.BlockSpec((1, ts, LANES), lambda i, j: (i, 0, 0)),
            tilen(cb), tilen(cb + 1), tilen(cb + 2),
            pl.BlockSpec((1, ts, LANES), lambda i, j: (i, nxt(j), 0)),
            pl.BlockSpec((1, ts, w), lambda i, j: (i, j, cb + 3)),
            pl.BlockSpec((taps, 3 * w), lambda i, j: (0, 0)),
            pl.BlockSpec((2 * heads, 2), lambda i, j: (0, 0)),
            pl.BlockSpec((1, dh), lambda i, j: (0, 0)),
        ],
        out_specs=pl.BlockSpec((1, ts, w), lambda i, j: (i, j, 0)),
        out_shape=jax.ShapeDtypeStruct((b, s, w), BF16),
        scratch_shapes=[pltpu.VMEM((HALO, 3 * w), F32), pltpu.VMEM((ts, 3 * w), F32),
                        pltpu.VMEM((ts, LANES), F32), pltpu.VMEM((npairs, 2 * heads, DN_TILE), F32),
                        pltpu.VMEM((heads, dh, dh), F32)],
        compiler_params=pltpu.CompilerParams(
            dimension_semantics=("parallel", "arbitrary"), vmem_limit_bytes=VMEM_LIMIT),
        name="deltanet",
    )(p3, p3, p3, ba3, p3, p3, p3, ba3, p3, w_short, hp, onw)


def _outproj_kernel(x_ref, mods_ref, yc_ref, yd_ref, wc_ref, wd_ref, o_ref, *, mod_idx):
    y = jnp.dot(yc_ref[...], wc_ref[...], preferred_element_type=F32)
    y = y + jnp.dot(yd_ref[...], wd_ref[...], preferred_element_type=F32)
    o_ref[...] = x_ref[...] + mods_ref[0, mod_idx:mod_idx + 1, :] * y


def _outproj(x2, mods, yc, yd, w_o, *, mod_idx, seq, tm=512):
    m, d = x2.shape
    half = w_o.shape[0] // 2
    assert yc.shape[1] == half and yd.shape[1] == half
    tm = _tile(seq, tm)
    tiles_per_batch = seq // tm
    return pl.pallas_call(
        functools.partial(_outproj_kernel, mod_idx=mod_idx),
        grid=(m // tm,),
        in_specs=[
            pl.BlockSpec((tm, d), lambda i: (i, 0)),
            pl.BlockSpec((1, N_MOD, d), lambda i: (i // tiles_per_batch, 0, 0)),
            pl.BlockSpec((tm, yc.shape[1]), lambda i: (i, 0)),
            pl.BlockSpec((tm, yd.shape[1]), lambda i: (i, 0)),
            pl.BlockSpec((half, d), lambda i: (0, 0)),
            pl.BlockSpec((half, d), lambda i: (1, 0)),
        ],
        out_specs=pl.BlockSpec((tm, d), lambda i: (i, 0)),
        out_shape=jax.ShapeDtypeStruct((m, d), F32),
        compiler_params=pltpu.CompilerParams(
            dimension_semantics=("parallel",), vmem_limit_bytes=VMEM_LIMIT),
        name="out_proj",
    )(x2, mods, yc, yd, w_o, w_o)


def kernel(x, c, w_ada, b_ada, ffn1_norm, ffn1_wg, ffn1_wu, ffn1_wd, mix_norm, w_in, w_dw, b_dw,
           conv_ln_w, conv_ln_b, w_short, a_log, dt_bias, dn_norm_w, w_out, ffn2_norm, ffn2_wg,
           ffn2_wu, ffn2_wd, final_norm):
    b, s, d = x.shape
    depth = w_ada.shape[0]
    heads = a_log.shape[1]
    dh = dn_norm_w.shape[1]
    cw = w_dw.shape[2]
    dnw = heads * dh
    n_main = 2 * cw + 4 * dnw
    assert w_in.shape[2] == n_main + 2 * heads and 2 * heads <= LANES and cw == dnw and depth >= 1

    x2 = x.reshape(b * s, d)
    c_pad = jnp.pad(c, ((0, (-b) % HALO), (0, 0)))
    row = lambda v: v.reshape(1, -1)
    for l in range(depth):
        mods = _mods(c_pad, w_ada[l], row(b_ada[l]))[:b].reshape(b, N_MOD, d)
        last = l == depth - 1

        x2 = _ffn(x2, mods, row(ffn1_norm[l]), ffn1_wg[l].astype(BF16), ffn1_wu[l].astype(BF16),
                  ffn1_wd[l].astype(BF16), None, mod_base=0, seq=s)

        w_main = w_in[l].astype(BF16)
        w_small = jnp.pad(w_in[l][:, n_main:], ((0, 0), (0, LANES - 2 * heads))).astype(BF16)
        later = [ffn2_wg[l], ffn2_wu[l], ffn2_wd[l], w_out[l]]
        p, ba, y_conv, wg2, wu2, wd2, wo = _inproj(
            x2, mods, row(mix_norm[l]), w_main, w_small, w_dw[l], row(b_dw[l]), row(conv_ln_w[l]),
            row(conv_ln_b[l]), later, mod_base=3, seq=s)
        p3 = p.reshape(b, s, 4 * dnw)
        ba3 = ba.reshape(b, s, LANES)

        zh = jnp.zeros((heads,), F32)
        hp = jnp.stack([jnp.concatenate([zh, a_log[l]]), jnp.concatenate([zh, dt_bias[l]])], axis=1)
        y_dn = _deltanet(p3, ba3, w_short[l], hp, row(dn_norm_w[l]), heads=heads, dh=dh, col0=0)

        x2 = _outproj(x2, mods, y_conv, y_dn.reshape(b * s, dnw),
                      wo, mod_idx=5, seq=s)

        x2 = _ffn(x2, mods, row(ffn2_norm[l]), wg2, wu2, wd2,
                  row(final_norm) if last else None, mod_base=6, seq=s)
    return x2.reshape(b, s, d)
```
